```python
import math
import jax
import jax.numpy as jnp
from jax import lax
import numpy as np

D_MODEL = 1024
BATCH = 16
SEQ = 256
DEPTH = 4
DEC_BATCH = 8
DEC_SEQ = 2048
PAST_LEN = 256

GRID_W = 64
A_HEADS = 4
A_DH = 64
A_DV = 128
B_HEADS = 8
B_DH = 64
NA_ROWS = 8
NA_COLS = 16
C_HEADS = 4
C_DK = 64
C_DV = 128
RET_CHUNK = 128
N_BRANCH = 3
N_EXPERTS = 16
EXPERT_FF = 1024
EC_FACTOR = 2
ROPE_BASE = 10000.0
EPS = 1e-6
Q_BLOCK = 128
NEG_INF = -1e30

A_QK = A_HEADS * 2 * A_DH
A_V = A_HEADS * A_DV
B_W = B_HEADS * B_DH
C_QK = C_HEADS * C_DK
C_V = C_HEADS * C_DV
D_IN = 2 * A_QK + A_V + 3 * B_W + 2 * C_QK + 2 * C_V + N_BRANCH * D_MODEL

kernel_name = 'hybrid_diffusion_trunk_step'


def _rmsnorm(x, g):
    x32 = x.astype(jnp.float32)
    y = x32 * lax.rsqrt(jnp.mean(x32 * x32, axis=-1, keepdims=True) + EPS)
    return y.astype(x.dtype) * g


def _modulation(cvec, w_mod, b_mod):
    m = jnp.einsum('bd,de->be', jax.nn.silu(cvec), w_mod) + b_mod
    return [t[:, None, :] for t in jnp.split(m, 6, axis=-1)]


def _split_projection(h, w_in):
    b, n, _ = h.shape
    sizes = [A_QK, A_QK, A_V, B_W, B_W, B_W, C_QK, C_QK, C_V, C_V]
    offsets = [int(o) for o in np.cumsum(sizes)]
    z = jnp.einsum('bnd,de->bne', h, w_in)
    aq, ak, av, bq, bk, bv, cq, ck, cv, cg, gates = jnp.split(z, offsets, axis=-1)
    return (aq.reshape(b, n, A_HEADS, 2, A_DH), ak.reshape(b, n, A_HEADS, 2, A_DH),
            av.reshape(b, n, A_HEADS, A_DV),
            bq.reshape(b, n, B_HEADS, B_DH), bk.reshape(b, n, B_HEADS, B_DH),
            bv.reshape(b, n, B_HEADS, B_DH),
            cq.reshape(b, n, C_HEADS, C_DK), ck.reshape(b, n, C_HEADS, C_DK) * (C_DK ** -0.5),
            cv.reshape(b, n, C_HEADS, C_DV), cg, gates.reshape(b, n, N_BRANCH, D_MODEL))


def _axial_rope(x, n):
    half = A_DH // 2
    quarter = half // 2
    freqs = ROPE_BASE ** (-jnp.arange(quarter, dtype=jnp.float32) * 2.0 / half)
    t = jnp.arange(n)
    row = (t // GRID_W).astype(jnp.float32)[:, None] * freqs
    col = (t % GRID_W).astype(jnp.float32)[:, None] * freqs
    ang = jnp.concatenate([row, row, col, col], axis=-1)
    shape = (1, n) + (1,) * (x.ndim - 3) + (A_DH,)
    cos = jnp.cos(ang).reshape(shape)
    sin = jnp.sin(ang).reshape(shape)
    x32 = x.astype(jnp.float32)
    a1, a2, b1, b2 = jnp.split(x32, 4, axis=-1)
    rot = jnp.concatenate([-a2, a1, -b2, b1], axis=-1)
    return (x32 * cos + rot * sin).astype(x.dtype)


def _sweep_query_blocks(q, fn):
    b, n = q.shape[:2]
    nb = n // Q_BLOCK
    qb = jnp.moveaxis(q.reshape((b, nb, Q_BLOCK) + q.shape[2:]), 1, 0)
    out = jnp.moveaxis(lax.map(fn, qb), 0, 1)
    return out.reshape((b, n) + out.shape[3:])


def _diff_attention(q, k, v, lam):
    scale = A_DH ** -0.5

    def block(qb):
        s = jnp.einsum('bqhid,bkhid->bhiqk', qb, k).astype(jnp.float32) * scale
        p = jax.nn.softmax(s, axis=-1)
        a = p[:, :, 0] - lam * p[:, :, 1]
        return jnp.einsum('bhqk,bkhe->bqhe', a.astype(v.dtype), v)
    return _sweep_query_blocks(q, block)


def _diff_lambda(lam, lam_init):
    lam = lam.astype(jnp.float32)
    return jnp.exp(jnp.sum(lam[0] * lam[1])) - jnp.exp(jnp.sum(lam[2] * lam[3])) + lam_init


def _diff_output(o, gain, lam_init):
    b, n = o.shape[:2]
    o32 = o.astype(jnp.float32)
    on = o32 * lax.rsqrt(jnp.mean(o32 * o32, axis=-1, keepdims=True) + EPS)
    return (on.reshape(b, n, A_V).astype(o.dtype) * gain) * (1.0 - lam_init)


def _softmax_attention(q, k, v):
    scale = q.shape[-1] ** -0.5

    def block(qb):
        s = jnp.einsum('bqhd,bkhd->bhqk', qb, k).astype(jnp.float32) * scale
        p = jax.nn.softmax(s, axis=-1)
        return jnp.einsum('bhqk,bkhd->bqhd', p.astype(v.dtype), v)
    return _sweep_query_blocks(q, block)


def _neighbourhood_attention(q, k, v, ck, cv, rpb):
    b, n, h, d = q.shape
    rows = n // GRID_W
    kr = min(NA_ROWS, rows)
    kc = NA_COLS
    band = 2 * kc
    ncb = GRID_W // kc
    scale = d ** -0.5
    q_row = jnp.arange(rows)
    key_rows = jnp.clip(q_row - kr // 2, 0, rows - kr)[:, None] + jnp.arange(kr)
    q_col = jnp.arange(ncb)[:, None] * kc + jnp.arange(kc)
    band_start = jnp.clip(jnp.arange(ncb) * kc - kc // 2, 0, GRID_W - band)
    key_cols = band_start[:, None] + jnp.arange(band)
    kidx = (key_rows[:, None, :, None] * GRID_W + key_cols[None, :, None, :]).reshape(rows, ncb, kr * band)
    kg = k[:, kidx]
    vg = v[:, kidx]
    qg = q.reshape(b, rows, ncb, kc, h, d)
    s_win = jnp.einsum('brcqhd,brckhd->bhrcqk', qg, kg).astype(jnp.float32) * scale
    win_start = jnp.clip(q_col - kc // 2, 0, GRID_W - kc)
    col_ok = ((key_cols[:, None, :] >= win_start[..., None])
              & (key_cols[:, None, :] < win_start[..., None] + kc))
    col_ok = jnp.broadcast_to(col_ok[:, :, None, :], (ncb, kc, kr, band)).reshape(ncb, kc, kr * band)
    dr = key_rows - q_row[:, None] + (NA_ROWS - 1)
    dc = jnp.clip(key_cols[:, None, :] - q_col[..., None] + (NA_COLS - 1), 0, 2 * NA_COLS - 2)
    bias = rpb[:, dr[:, None, None, :, None], dc[None, :, :, None, :]].reshape(h, rows, ncb, kc, kr * band)
    s_win = jnp.where(col_ok, s_win + bias[None].astype(jnp.float32), NEG_INF)
    s_ctx = jnp.einsum('brcqhd,blhd->bhrcql', qg, ck).astype(jnp.float32) * scale
    nwin = kr * band
    p = jax.nn.softmax(jnp.concatenate([s_win, s_ctx], axis=-1), axis=-1).astype(v.dtype)
    o = (jnp.einsum('bhrcqk,brckhd->brcqhd', p[..., :nwin], vg)
         + jnp.einsum('bhrcql,blhd->brcqhd', p[..., nwin:], cv))
    return o.reshape(b, n, h, d)


def _retention_scan(q, k, v, log_gamma, s0):
    b, n, h, dk = q.shape
    dv = v.shape[-1]
    t = RET_CHUNK
    nc = n // t
    f32 = jnp.float32

    def chunks(x):
        return jnp.moveaxis(x.astype(f32).reshape((b, nc, t) + x.shape[2:]), 1, 0)
    pos = jnp.arange(t, dtype=f32)
    diff = pos[:, None] - pos[None, :]
    d_in = jnp.where((diff >= 0)[None],
                     jnp.exp(log_gamma[:, None, None] * jnp.maximum(diff, 0.0)[None]), 0.0)
    d_cross = jnp.exp(log_gamma[None, :] * (pos[:, None] + 1.0))[None, :, :, None]
    d_upd = jnp.exp(log_gamma[None, :] * (t - 1.0 - pos[:, None]))[None, :, :, None]
    g_chunk = jnp.exp(log_gamma * t)[None, :, None, None]

    def step(s, inp):
        qb, kb, vb = inp
        att = jnp.einsum('bihd,bjhd->bhij', qb, kb) * d_in[None]
        o = (jnp.einsum('bhij,bjhe->bihe', att, vb)
             + jnp.einsum('bihd,bhde->bihe', qb, s) * d_cross)
        s_new = g_chunk * s + jnp.einsum('bjhd,bjhe->bhde', kb * d_upd, vb)
        return s_new, o
    s_fin, o = lax.scan(step, s0.astype(f32), (chunks(q), chunks(k), chunks(v)))
    return jnp.moveaxis(o, 0, 1).reshape(b, n, h, dv), s_fin


def _bidirectional_retention(q, k, v, log_gamma, s0_fwd, s0_bwd):
    o_f, s_f = _retention_scan(q, k, v, log_gamma[0], s0_fwd)
    o_b, s_b = _retention_scan(jnp.flip(q, 1), jnp.flip(k, 1), jnp.flip(v, 1), log_gamma[1], s0_bwd)
    return o_f + jnp.flip(o_b, 1), s_f, s_b


def _retention_output(o, g, gain):
    b, n = o.shape[:2]
    mu = jnp.mean(o, axis=-1, keepdims=True)
    var = jnp.mean(jnp.square(o - mu), axis=-1, keepdims=True)
    on = ((o - mu) * lax.rsqrt(var + EPS)).reshape(b, n, C_V).astype(g.dtype) * gain
    return jax.nn.silu(g) * on


def _merge(gates, a_o, b_o, c_o, w_pa, w_pb, w_pc, w_o):
    b, n = a_o.shape[:2]
    br = jnp.stack([jnp.einsum('bne,ed->bnd', a_o.reshape(b, n, A_V), w_pa),
                    jnp.einsum('bne,ed->bnd', b_o.reshape(b, n, B_W), w_pb),
                    jnp.einsum('bne,ed->bnd', c_o, w_pc)], axis=2)
    merged = jnp.sum(jax.nn.sigmoid(gates) * br, axis=2)
    return jnp.einsum('bnd,de->bne', merged, w_o)


def _expert_choice_ffn(h, w_router, w_gate, w_up, w_down):
    b, n, d = h.shape
    cap = EC_FACTOR * n // N_EXPERTS
    aff = jax.nn.softmax(jnp.einsum('bnd,de->bne', h, w_router).astype(jnp.float32), axis=-1)
    g, idx = lax.top_k(jnp.swapaxes(aff, 1, 2), cap)
    xg = jax.vmap(lambda hb, ib: hb[ib])(h, idx)
    hid = jax.nn.silu(jnp.einsum('becd,edf->becf', xg, w_gate)) * jnp.einsum('becd,edf->becf', xg, w_up)
    out = jnp.einsum('becf,efd->becd', hid, w_down) * g[..., None].astype(h.dtype)
    return jax.vmap(lambda ob, ib: jnp.zeros((n, d), ob.dtype).at[ib.reshape(-1)].add(ob.reshape(-1, d)))(out, idx)


def setup_inputs(seed: int = 0) -> dict:
    key = jax.random.key(seed)
    ks = iter(jax.random.split(key, 40))
    f32 = jnp.float32

    def nrm(shape, scale):
        return jax.random.normal(next(ks), shape, f32) * scale
    gam = 1.0 - 2.0 ** (-5.0 - np.arange(C_HEADS))
    decay_logit = np.log(gam / (1.0 - gam)).astype(np.float32)
    return {
        'x_prompt': nrm((BATCH, SEQ, D_MODEL), 1.0),
        'x_sample': nrm((DEC_BATCH, DEC_SEQ, D_MODEL), 1.0),
        'cache_a_k': nrm((DEC_BATCH, DEPTH, PAST_LEN, A_HEADS, 2, A_DH), 1.0),
        'cache_a_v': nrm((DEC_BATCH, DEPTH, PAST_LEN, A_HEADS, A_DV), 1.0),
        'cache_b_k': nrm((DEC_BATCH, DEPTH, PAST_LEN, B_HEADS, B_DH), 1.0),
        'cache_b_v': nrm((DEC_BATCH, DEPTH, PAST_LEN, B_HEADS, B_DH), 1.0),
        'state_c_fwd': nrm((DEC_BATCH, DEPTH, C_HEADS, C_DK, C_DV), 0.5),
        'state_c_bwd': nrm((DEC_BATCH, DEPTH, C_HEADS, C_DK, C_DV), 0.5),
        'c': nrm((DEC_BATCH, D_MODEL), 1.0),
        'c_ctx': nrm((D_MODEL,), 1.0),
        'w_mod': nrm((DEPTH, D_MODEL, 6 * D_MODEL), 0.5 * D_MODEL ** -0.5),
        'b_mod': nrm((DEPTH, 6 * D_MODEL), 0.02),
        'norm_mix': 1.0 + nrm((DEPTH, D_MODEL), 0.02),
        'norm_ffn': 1.0 + nrm((DEPTH, D_MODEL), 0.02),
        'w_in': nrm((DEPTH, D_MODEL, D_IN), D_MODEL ** -0.5),
        'diff_lambda': nrm((DEPTH, 4, A_DH), 0.1),
        'diff_norm': 1.0 + nrm((DEPTH, A_V), 0.02),
        'na_rpb': nrm((DEPTH, B_HEADS, 2 * NA_ROWS - 1, 2 * NA_COLS - 1), 0.1),
        'ret_decay': jnp.asarray(decay_logit)[None, None, :] + nrm((DEPTH, 2, C_HEADS), 0.1),
        'ret_norm': 1.0 + nrm((DEPTH, C_V), 0.02),
        'w_proj_a': nrm((DEPTH, A_V, D_MODEL), A_V ** -0.5),
        'w_proj_b': nrm((DEPTH, B_W, D_MODEL), B_W ** -0.5),
        'w_proj_c': nrm((DEPTH, C_V, D_MODEL), C_V ** -0.5),
        'w_out': nrm((DEPTH, D_MODEL, D_MODEL), D_MODEL ** -0.5),
        'w_router': nrm((DEPTH, D_MODEL, N_EXPERTS), D_MODEL ** -0.5),
        'w_exp_gate': nrm((DEPTH, N_EXPERTS, D_MODEL, EXPERT_FF), D_MODEL ** -0.5),
        'w_exp_up': nrm((DEPTH, N_EXPERTS, D_MODEL, EXPERT_FF), D_MODEL ** -0.5),
        'w_exp_down': nrm((DEPTH, N_EXPERTS, EXPERT_FF, D_MODEL), EXPERT_FF ** -0.5),
        'norm_final': 1.0 + nrm((D_MODEL,), 0.02),
    }


def reference(x_prompt, x_sample, cache_a_k, cache_a_v, cache_b_k, cache_b_v, state_c_fwd, state_c_bwd,
              c, c_ctx, w_mod, b_mod, norm_mix, norm_ffn, w_in, diff_lambda, diff_norm, na_rpb,
              ret_decay, ret_norm, w_proj_a, w_proj_b, w_proj_c, w_out, w_router, w_exp_gate,
              w_exp_up, w_exp_down, norm_final):
    xc = x_prompt
    xl = x_sample
    n_lat = xl.shape[1]
    new_ak, new_av, new_bk, new_bv, new_sf, new_sb = [], [], [], [], [], []
    for l in range(DEPTH):
        lam_init = 0.8 - 0.6 * math.exp(-0.3 * l)
        log_gamma = jax.nn.log_sigmoid(ret_decay[l].astype(jnp.float32))
        lam = _diff_lambda(diff_lambda[l], lam_init)
        mc = _modulation(c_ctx[None], w_mod[l], b_mod[l])
        ml = _modulation(c, w_mod[l], b_mod[l])

        h = _rmsnorm(xc, norm_mix[l]) * (1.0 + mc[1]) + mc[0]
        aq, ak, av, bq, bk, bv, cq, ck, cv, cg, gates = _split_projection(h, w_in[l])
        a_o = _diff_output(_diff_attention(aq, ak, av, lam), diff_norm[l], lam_init)
        b_o = _softmax_attention(bq, bk, bv)
        zero_state = jnp.zeros((xc.shape[0], C_HEADS, C_DK, C_DV), jnp.float32)
        c_raw, s_f, s_b = _bidirectional_retention(cq, ck, cv, log_gamma, zero_state, zero_state)
        c_o = _retention_output(c_raw, cg, ret_norm[l])
        xc = xc + mc[2] * _merge(gates, a_o, b_o, c_o, w_proj_a[l], w_proj_b[l], w_proj_c[l], w_out[l])
        h = _rmsnorm(xc, norm_ffn[l]) * (1.0 + mc[4]) + mc[3]
        xc = xc + mc[5] * _expert_choice_ffn(h, w_router[l], w_exp_gate[l], w_exp_up[l], w_exp_down[l])
        new_ak.append(ak)
        new_av.append(av)
        new_bk.append(bk)
        new_bv.append(bv)
        new_sf.append(s_f.astype(xc.dtype))
        new_sb.append(s_b.astype(xc.dtype))

        h = _rmsnorm(xl, norm_mix[l]) * (1.0 + ml[1]) + ml[0]
        aq, ak, av, bq, bk, bv, cq, ck, cv, cg, gates = _split_projection(h, w_in[l])
        aq = _axial_rope(aq, n_lat)
        ak = _axial_rope(ak, n_lat)
        k_all = jnp.concatenate([ak, cache_a_k[:, l].astype(ak.dtype)], axis=1)
        v_all = jnp.concatenate([av, cache_a_v[:, l].astype(av.dtype)], axis=1)
        a_o = _diff_output(_diff_attention(aq, k_all, v_all, lam), diff_norm[l], lam_init)
        b_o = _neighbourhood_attention(bq, bk, bv, cache_b_k[:, l].astype(bq.dtype),
                                       cache_b_v[:, l].astype(bv.dtype), na_rpb[l])
        c_raw, _, _ = _bidirectional_retention(cq, ck, cv, log_gamma, state_c_fwd[:, l], state_c_bwd[:, l])
        c_o = _retention_output(c_raw, cg, ret_norm[l])
        xl = xl + ml[2] * _merge(gates, a_o, b_o, c_o, w_proj_a[l], w_proj_b[l], w_proj_c[l], w_out[l])
        h = _rmsnorm(xl, norm_ffn[l]) * (1.0 + ml[4]) + ml[3]
        xl = xl + ml[5] * _expert_choice_ffn(h, w_router[l], w_exp_gate[l], w_exp_up[l], w_exp_down[l])

    y_prompt = _rmsnorm(xc, norm_final)
    y_sample = _rmsnorm(xl, norm_final)
    return (y_prompt, y_sample, jnp.stack(new_ak, axis=1), jnp.stack(new_av, axis=1),
            jnp.stack(new_bk, axis=1), jnp.stack(new_bv, axis=1),
            jnp.stack(new_sf, axis=1), jnp.stack(new_sb, axis=1))
```

```python
import functools
import math

import numpy as np
import jax
import jax.numpy as jnp
from jax import lax
from jax.experimental import pallas as pl
from jax.experimental.pallas import tpu as pltpu

F32 = jnp.float32
BF16 = jnp.bfloat16
I32 = jnp.int32

D = 1024
DEPTH = 4
GRID_W = 64
A_DH = 64
N_EXPERTS = 16
EXPERT_FF = 1024
RET_CHUNK = 128
ROPE_BASE = 10000.0
EPS = 1e-6
NEG_INF = -1e30

Z_W = 4608
G_W = 3072
PROJ_TN = 512
N_ZT = Z_W // PROJ_TN
N_GT = G_W // PROJ_TN
COL_AQ, COL_AK, COL_AV = 0, 4, 8
COL_BQ, COL_BK, COL_BV = 12, 16, 20
COL_CQ, COL_CK, COL_CV, COL_CG = 24, 26, 28, 32

NA_QR = 4
NA_KR = 12
NA_HALF = 4
NA_ROWS = 8
NA_COLS = 16

TOK_TILE = 2048
SLOTS = 256
VMEM_LIMIT = 56 * 1024 * 1024


def _params(sem):
    return pltpu.CompilerParams(dimension_semantics=sem, vmem_limit_bytes=VMEM_LIMIT)


def _dot(a, b):
    return jnp.dot(a, b, preferred_element_type=F32)


def _dot_nt(a, b):
    return lax.dot_general(a, b, (((1,), (1,)), ((), ())), preferred_element_type=F32)


def _half_masks(dtype, scale=1.0):
    lane = lax.broadcasted_iota(I32, (1, 128), 1)
    lo = jnp.where(lane < 64, scale, 0.0).astype(dtype)
    hi = jnp.where(lane >= 64, scale, 0.0).astype(dtype)
    return lo, hi


def _mod_kernel(c_ref, w_ref, b_ref, o_ref):
    c = c_ref[...]
    s = (c * jax.nn.sigmoid(c)).astype(BF16)
    o_ref[0] = _dot(s, w_ref[0].astype(BF16)) + b_ref[0]


def _modulation(cvec, w_mod, b_mod):
    tn = 1536
    return pl.pallas_call(
        _mod_kernel,
        grid=(DEPTH, 6 * D // tn),
        in_specs=[pl.BlockSpec((16, D), lambda l, j: (0, 0)),
                  pl.BlockSpec((1, D, tn), lambda l, j: (l, 0, j)),
                  pl.BlockSpec((1, 1, tn), lambda l, j: (l, 0, j))],
        out_specs=pl.BlockSpec((1, 16, tn), lambda l, j: (l, 0, j)),
        out_shape=jax.ShapeDtypeStruct((DEPTH, 16, 6 * D), F32),
        compiler_params=_params(("arbitrary", "arbitrary")),
        name="modulation",
    )(cvec, w_mod, b_mod.reshape(DEPTH, 1, 6 * D))


def _mod_spec(l, which, row_of_tile, ngrid):
    if ngrid == 1:
        return pl.BlockSpec((1, 1, D), lambda i: (l * 16 + row_of_tile(i), 0, which))
    return pl.BlockSpec((1, 1, D), lambda i, j: (l * 16 + row_of_tile(i), 0, which))


def _rope_tables(n):
    half = A_DH // 2
    quarter = half // 2
    freqs = ROPE_BASE ** (-np.arange(quarter, dtype=np.float64) * 2.0 / half)
    t = np.arange(n)
    row = (t // GRID_W).astype(np.float64)[:, None] * freqs
    col = (t % GRID_W).astype(np.float64)[:, None] * freqs
    ang = np.concatenate([row, row, col, col], axis=-1)
    ang = np.concatenate([ang, ang], axis=-1)
    first = (np.arange(128) % 32) < 16
    cos = np.cos(ang)
    sin = np.where(first[None, :], -np.sin(ang), np.sin(ang))
    return jnp.asarray(cos, F32), jnp.asarray(sin, F32)


def _proj_kernel(*refs, tm, rope, emit_cache):
    x_ref, sh_ref, sc_ref, g_ref, w_ref = refs[:5]
    k = 5
    if rope:
        cos_ref, sin_ref = refs[k:k + 2]
        k += 2
    z_ref, gt_ref = refs[k:k + 2]
    k += 2
    if emit_cache:
        cache_refs = refs[k:k + 4]
        k += 4
    h_scr = refs[k]
    j = pl.program_id(1)

    @pl.when(j == 0)
    def _():
        x = x_ref[...]
        y = x * lax.rsqrt(jnp.mean(x * x, axis=-1, keepdims=True) + EPS) * g_ref[0]
        h_scr[...] = (y * (1.0 + sc_ref[0]) + sh_ref[0]).astype(BF16)

    z = _dot(h_scr[...], w_ref[0].astype(BF16))

    if rope:
        @pl.when(j < 2)
        def _():
            lane = lax.broadcasted_iota(I32, (tm, 128), 1)
            first = (lane & 31) < 16
            for cc in range(PROJ_TN // 128):
                zc = z[:, cc * 128:(cc + 1) * 128]
                rot = jnp.where(first, pltpu.roll(zc, 112, 1), pltpu.roll(zc, 16, 1))
                z_ref[:, cc * 128:(cc + 1) * 128] = (zc * cos_ref[...] + rot * sin_ref[...]).astype(BF16)

        @pl.when((j >= 2) & (j < N_ZT))
        def _():
            z_ref[...] = z.astype(BF16)
    else:
        @pl.when(j < N_ZT)
        def _():
            z_ref[...] = z.astype(BF16)

    @pl.when(j >= N_ZT)
    def _():
        gt_ref[...] = z.astype(BF16)

    if emit_cache:
        for tile, ref in zip((1, 2, 4, 5), cache_refs):
            @pl.when(j == tile)
            def _(ref=ref):
                ref[...] = z.reshape(tm // 256, 1, 256, PROJ_TN)


def _project(x, mods, norm_g, w_in, l, *, tm, row_of_tile, rope_tabs=None, caches=None, n_ctx=None):
    t = x.shape[0]
    nt = t // tm
    rope = rope_tabs is not None
    emit_cache = n_ctx is not None
    in_specs = [pl.BlockSpec((tm, D), lambda i, j: (i, 0)),
                _mod_spec(l, 0, row_of_tile, 2), _mod_spec(l, 1, row_of_tile, 2),
                pl.BlockSpec((1, 1, D), lambda i, j: (l, 0, 0)),
                pl.BlockSpec((1, D, PROJ_TN), lambda i, j: (l, 0, j))]
    args = [x, mods, mods, norm_g, w_in]
    if rope:
        in_specs += [pl.BlockSpec((tm, 128), lambda i, j: (0, 0))] * 2
        args += list(rope_tabs)
    out_specs = [pl.BlockSpec((tm, PROJ_TN), lambda i, j: (i, jnp.minimum(j, N_ZT - 1))),
                 pl.BlockSpec((tm, PROJ_TN), lambda i, j: (i, jnp.maximum(j - N_ZT, 0)))]
    out_shape = [jax.ShapeDtypeStruct((t, Z_W), BF16), jax.ShapeDtypeStruct((t, G_W), BF16)]
    aliases = {}
    if emit_cache:
        nb = tm // 256
        out_specs += [pl.BlockSpec((nb, 1, 256, PROJ_TN), lambda i, j: (i, l, 0, 0))] * 4
        out_shape += [jax.ShapeDtypeStruct((n_ctx, DEPTH, 256, PROJ_TN), F32)] * 4
        if caches is not None:
            base = len(args)
            in_specs += [pl.BlockSpec(memory_space=pl.ANY)] * 4
            args += list(caches)
            aliases = {base + a: 2 + a for a in range(4)}
    kern = functools.partial(_proj_kernel, tm=tm, rope=rope, emit_cache=emit_cache)
    if emit_cache and caches is not None:
        def kern(*refs):
            n_in = len(args)
            return _proj_kernel(*refs[:n_in - 4], *refs[n_in:], tm=tm, rope=rope, emit_cache=True)
    return pl.pallas_call(
        kern,
        grid=(nt, N_ZT + N_GT),
        in_specs=in_specs, out_specs=out_specs, out_shape=out_shape,
        scratch_shapes=[pltpu.VMEM((tm, D), BF16)],
        input_output_aliases=aliases,
        compiler_params=_params(("arbitrary", "arbitrary")),
        name="in_proj_ctx" if emit_cache else "in_proj_lat",
    )(*args)


def _diff_lambda(dl_ref, lam_init):
    dl = dl_ref[0]
    la = jnp.sum(dl[0:1] * dl[1:2], axis=-1, keepdims=True)
    lb = jnp.sum(dl[2:3] * dl[3:4], axis=-1, keepdims=True)
    return jnp.exp(la) - jnp.exp(lb) + lam_init


def _diff_lat_kernel(q_ref, k_ref, v_ref, ck_ref, cv_ref, dl_ref, gain_ref, o_ref, vt_s, *, n, lam_init):
    @pl.when(pl.program_id(2) == 0)
    def _():
        vt_s[:, :n] = v_ref[0].astype(F32).T.astype(BF16)
        vt_s[:, n:] = cv_ref[0, 0].T.astype(BF16)

    lo, hi = _half_masks(BF16, A_DH ** -0.5)
    q = q_ref[0]
    kk = k_ref[0]
    ck = ck_ref[0, 0].astype(BF16)
    lam = _diff_lambda(dl_ref, lam_init)
    es, dens = [], []
    for msk in (lo, hi):
        qh = q * msk
        s = _dot_nt(kk, qh)
        s2 = _dot_nt(ck, qh)
        m = jnp.maximum(jnp.max(s, axis=0, keepdims=True), jnp.max(s2, axis=0, keepdims=True))
        e = jnp.exp(s - m)
        e2 = jnp.exp(s2 - m)
        es.append((e, e2))
        dens.append(jnp.sum(e, axis=0, keepdims=True) + jnp.sum(e2, axis=0, keepdims=True))
    r = lam * dens[0] / dens[1]
    a = (es[0][0] - r * es[1][0]).astype(BF16)
    a2 = (es[0][1] - r * es[1][1]).astype(BF16)
    ot = (_dot(vt_s[:, :n], a) + _dot(vt_s[:, n:], a2)) / dens[0]
    on = ot * lax.rsqrt(jnp.mean(ot * ot, axis=0, keepdims=True) + EPS)
    o_ref[0] = ((on * gain_ref[0]) * (1.0 - lam_init)).T.astype(BF16)


def _diff_attention_lat(z3, cache_k, cache_v, diff_lambda, diff_norm_col, l, *, tq):
    b, n, _ = z3.shape
    past = cache_k.shape[2]
    lam_init = 0.8 - 0.6 * math.exp(-0.3 * l)
    return pl.pallas_call(
        functools.partial(_diff_lat_kernel, n=n, lam_init=lam_init),
        grid=(b, 4, n // tq),
        in_specs=[pl.BlockSpec((1, tq, 128), lambda bi, h, qi: (bi, qi, COL_AQ + h)),
                  pl.BlockSpec((1, n, 128), lambda bi, h, qi: (bi, 0, COL_AK + h)),
                  pl.BlockSpec((1, n, 128), lambda bi, h, qi: (bi, 0, COL_AV + h)),
                  pl.BlockSpec((1, 1, past, 128), lambda bi, h, qi: (bi, l, 0, h)),
                  pl.BlockSpec((1, 1, past, 128), lambda bi, h, qi: (bi, l, 0, h)),
                  pl.BlockSpec((1, 4, A_DH), lambda bi, h, qi: (l, 0, 0)),
                  pl.BlockSpec((1, 128, 1), lambda bi, h, qi: (l, h, 0))],
        out_specs=pl.BlockSpec((1, tq, 128), lambda bi, h, qi: (bi, qi, h)),
        out_shape=jax.ShapeDtypeStruct((b, n, 512), BF16),
        scratch_shapes=[pltpu.VMEM((128, n + past), BF16)],
        compiler_params=_params(("arbitrary",) * 3),
        name="diff_attn_lat",
    )(z3, z3, z3, cache_k, cache_v, diff_lambda, diff_norm_col)


def _ctx_attn_kernel(aq_ref, ak_ref, av_ref, bq_ref, bk_ref, bv_ref, dl_ref, gain_ref, ao_ref, bo_ref,
                     *, lam_init):
    lo, hi = _half_masks(BF16, A_DH ** -0.5)
    one_lo, one_hi = _half_masks(BF16)
    lam = _diff_lambda(dl_ref, lam_init)
    for h in range(4):
        hs = slice(h * 128, h * 128 + 128)
        q = aq_ref[0, :, hs]
        kk = ak_ref[0, :, hs]
        v = av_ref[0, :, hs]
        outs = []
        for msk in (lo, hi):
            s = _dot_nt(q * msk, kk)
            e = jnp.exp(s - jnp.max(s, axis=-1, keepdims=True))
            outs.append(_dot(e.astype(BF16), v) / jnp.sum(e, axis=-1, keepdims=True))
        o = outs[0] - lam * outs[1]
        on = o * lax.rsqrt(jnp.mean(o * o, axis=-1, keepdims=True) + EPS)
        ao_ref[0, :, hs] = ((on * gain_ref[0, :, hs]) * (1.0 - lam_init)).astype(BF16)

        q = bq_ref[0, :, hs]
        kk = bk_ref[0, :, hs]
        v = bv_ref[0, :, hs]
        acc = None
        for msk, vmsk in ((lo, one_lo), (hi, one_hi)):
            s = _dot_nt(q * msk, kk)
            e = jnp.exp(s - jnp.max(s, axis=-1, keepdims=True))
            o = _dot(e.astype(BF16), v * vmsk) / jnp.sum(e, axis=-1, keepdims=True)
            acc = o if acc is None else acc + o
        bo_ref[0, :, hs] = acc.astype(BF16)


def _ctx_attention(z3, diff_lambda, diff_norm, l):
    b, n, _ = z3.shape
    lam_init = 0.8 - 0.6 * math.exp(-0.3 * l)
    col = lambda c: pl.BlockSpec((1, n, 512), lambda bi: (bi, 0, c))
    return pl.pallas_call(
        functools.partial(_ctx_attn_kernel, lam_init=lam_init),
        grid=(b,),
        in_specs=[col(0), col(1), col(2), col(3), col(4), col(5),
                  pl.BlockSpec((1, 4, A_DH), lambda bi: (l, 0, 0)),
                  pl.BlockSpec((1, 1, 512), lambda bi: (l, 0, 0))],
        out_specs=[pl.BlockSpec((1, n, 512), lambda bi: (bi, 0, 0))] * 2,
        out_shape=[jax.ShapeDtypeStruct((b, n, 512), BF16)] * 2,
        compiler_params=_params(("arbitrary",)),
        name="attn_ctx",
    )(z3, z3, z3, z3, z3, z3, diff_lambda, diff_norm)


def _na_table_kernel(rpb_ref, o_ref):
    l = pl.program_id(0)
    h = pl.program_id(1)
    nrel_r = 2 * NA_ROWS - 1
    nrel_c = 2 * NA_COLS - 1
    base = (l * 8 + h) * (nrel_r * nrel_c)
    kc = lax.broadcasted_iota(I32, (GRID_W, GRID_W), 0)
    qc = lax.broadcasted_iota(I32, (GRID_W, GRID_W), 1)
    dcm = jnp.clip(kc - qc + (NA_COLS - 1), 0, nrel_c - 1)
    ws = jnp.clip(qc - NA_COLS // 2, 0, GRID_W - NA_COLS)
    col_ok = (kc >= ws) & (kc < ws + NA_COLS)
    neg = jnp.full((GRID_W, GRID_W), NEG_INF, F32)
    tabs = []
    for dr in range(nrel_r):
        acc = jnp.zeros((GRID_W, GRID_W), F32)
        for dc in range(nrel_c):
            acc = jnp.where(dcm == dc, rpb_ref[base + dr * nrel_c + dc], acc)
        tabs.append(jnp.where(col_ok, acc, neg))
    for typ in range(3):
        for i in range(NA_QR):
            for j in range(NA_KR):
                if typ == 0:
                    valid, dr = j < NA_ROWS, j - i + 7
                elif typ == 1:
                    valid, dr = i <= j < i + NA_ROWS, j - i + 3
                else:
                    valid, dr = j >= 4, j - i - 1
                o_ref[0, 0, typ, j * 64:(j + 1) * 64, i * 64:(i + 1) * 64] = tabs[dr] if valid else neg


def _na_tables(na_rpb):
    return pl.pallas_call(
        _na_table_kernel,
        grid=(DEPTH, 8),
        in_specs=[pl.BlockSpec(memory_space=pltpu.SMEM)],
        out_specs=pl.BlockSpec((1, 1, 3, NA_KR * 64, NA_QR * 64), lambda l, h: (l, h, 0, 0, 0)),
        out_shape=jax.ShapeDtypeStruct((DEPTH, 8, 3, NA_KR * 64, NA_QR * 64), F32),
        compiler_params=_params(("arbitrary",) * 2),
        name="na_bias_tables",
    )(na_rpb.reshape(-1))


def _na_kernel(q_ref, k_ref, v_ref, ck_ref, cv_ref, tab_ref, o_ref, vt_s, cvt_s, *, n):
    rb = pl.program_id(2)
    nrb = pl.num_programs(2)

    @pl.when(rb == 0)
    def _():
        for pr in range(2):
            ps = slice(pr * 128, pr * 128 + 128)
            vt = v_ref[0, :, ps].astype(F32).T
            for c in range(n // 128):
                vt_s[pr, c] = vt[:, c * 128:(c + 1) * 128].astype(BF16)
            cvt_s[pr] = cv_ref[0, 0, :, ps].T.astype(BF16)

    start = jnp.clip(rb * NA_QR - NA_HALF, 0, 32 - NA_KR)
    typ = jnp.where(rb == 0, 0, jnp.where(rb == nrb - 1, 2, 1))
    off = pl.multiple_of(start * GRID_W, 256)
    c0 = start // 2
    nchunk = NA_KR * GRID_W // 128
    lo, hi = _half_masks(BF16, 0.125)
    for pr in range(2):
        ps = slice(pr * 128, pr * 128 + 128)
        q = q_ref[0, :, ps]
        kw = k_ref[0, pl.ds(off, NA_KR * GRID_W), ps]
        ck = ck_ref[0, 0, :, ps].astype(BF16)
        vtw = jnp.concatenate([vt_s[pr, c0 + c] for c in range(nchunk)], axis=1)
        cvt = cvt_s[pr]
        outs = []
        for half, msk in enumerate((lo, hi)):
            qh = q * msk
            s = _dot_nt(kw, qh) + tab_ref[0, pr * 2 + half, typ]
            s2 = _dot_nt(ck, qh)
            m = jnp.maximum(jnp.max(s, axis=0, keepdims=True), jnp.max(s2, axis=0, keepdims=True))
            e = jnp.exp(s - m)
            e2 = jnp.exp(s2 - m)
            den = jnp.sum(e, axis=0, keepdims=True) + jnp.sum(e2, axis=0, keepdims=True)
            hr = slice(half * 64, half * 64 + 64)
            outs.append((_dot(vtw[hr], e.astype(BF16)) + _dot(cvt[hr], e2.astype(BF16))) / den)
        o_ref[0, :, ps] = jnp.concatenate(outs, axis=0).T.astype(BF16)


def _na_attention(z3, cache_k, cache_v, tables, l):
    b, n, _ = z3.shape
    past = cache_k.shape[2]
    tq = NA_QR * GRID_W
    return pl.pallas_call(
        functools.partial(_na_kernel, n=n),
        grid=(2, b, n // tq),
        in_specs=[pl.BlockSpec((1, tq, 256), lambda pp, bi, rb: (bi, rb, COL_BQ // 2 + pp)),
                  pl.BlockSpec((1, n, 256), lambda pp, bi, rb: (bi, 0, COL_BK // 2 + pp)),
                  pl.BlockSpec((1, n, 256), lambda pp, bi, rb: (bi, 0, COL_BV // 2 + pp)),
                  pl.BlockSpec((1, 1, past, 256), lambda pp, bi, rb: (bi, l, 0, pp)),
                  pl.BlockSpec((1, 1, past, 256), lambda pp, bi, rb: (bi, l, 0, pp)),
                  pl.BlockSpec((1, 4, 3, NA_KR * 64, tq), lambda pp, bi, rb: (l, pp, 0, 0, 0))],
        out_specs=pl.BlockSpec((1, tq, 256), lambda pp, bi, rb: (bi, rb, pp)),
        out_shape=jax.ShapeDtypeStruct((b, n, 512), BF16),
        scratch_shapes=[pltpu.VMEM((2, n // 128, 128, 128), BF16), pltpu.VMEM((2, 128, past), BF16)],
        compiler_params=_params(("arbitrary",) * 3),
        name="na_attn_lat",
    )(z3, z3, z3, cache_k, cache_v, tables)


RET_UNROLL = 4


def _ret_kernel(*refs, **kw):
    for h in range(4):
        _ret_head(*refs, h=h, **kw)


def _ret_head(*refs, h, n, l, has_state, emit_state):
    rd_ref, q_ref, k_ref, v_ref, g_ref, gain_ref = refs[:6]
    i = 6
    if has_state:
        s0f_ref, s0b_ref = refs[i:i + 2]
        i += 2
    o_ref = refs[i]
    i += 1
    if emit_state:
        sf_ref, sb_ref = refs[i:i + 2]
        i += 2
    uf, ub = refs[i:i + 2]
    t = RET_CHUNK
    nc = n // t
    unroll = min(nc, RET_UNROLL)
    half = h % 2
    pair = slice((h // 2) * 128, (h // 2) * 128 + 128)
    head = slice(h * 128, h * 128 + 128)

    def log_gamma(direction):
        x = jnp.full((1, 1), rd_ref[l * 8 + direction * 4 + h], F32)
        return jnp.minimum(x, 0.0) - jnp.log1p(jnp.exp(-jnp.abs(x)))

    lgf = log_gamma(0)
    lgb = log_gamma(1)
    lane = lax.broadcasted_iota(I32, (1, 128), 1)
    hmask = jnp.where((lane >= 64) if half else (lane < 64), 1.0, 0.0)
    ri = lax.broadcasted_iota(I32, (t, t), 0).astype(F32)
    ci = lax.broadcasted_iota(I32, (t, t), 1).astype(F32)
    diff = ri - ci
    dmat = jnp.where(diff > 0, jnp.exp(lgf * jnp.maximum(diff, 0.0)),
                     jnp.where(diff < 0, jnp.exp(lgb * jnp.maximum(-diff, 0.0)), 2.0))
    pos_r = lax.broadcasted_iota(I32, (1, t), 1).astype(F32)
    pos_c = lax.broadcasted_iota(I32, (t, 1), 0).astype(F32)
    upd_f = jnp.exp(lgf * (t - 1.0 - pos_r))
    upd_b = jnp.exp(lgb * pos_r)
    cross_f = jnp.exp(lgf * (pos_c + 1.0))
    cross_b = jnp.exp(lgb * (t - pos_c))
    gch_f = jnp.exp(lgf * t)
    gch_b = jnp.exp(lgb * t)
    kscale = hmask * (64 ** -0.5)

    def chunk(ref, c, lanes):
        return ref[0, pl.ds(pl.multiple_of(c * t, t), t), lanes]

    def summarise(c, carry):
        kt = (chunk(k_ref, c, pair).astype(F32) * kscale).T
        vc = chunk(v_ref, c, head)
        uf[c] = _dot((kt * upd_f).astype(BF16), vc)
        ub[c] = _dot((kt * upd_b).astype(BF16), vc)
        return carry
    lax.fori_loop(0, nc, summarise, 0, unroll=unroll)

    if has_state:
        s0f = jnp.concatenate([s0f_ref[0, 0, h], s0f_ref[0, 0, h]], axis=0)
        s0b = jnp.concatenate([s0b_ref[0, 0, h], s0b_ref[0, 0, h]], axis=0)
    else:
        s0f = jnp.zeros((128, 128), F32)
        s0b = s0f

    def scan_f(c, s):
        u = uf[c]
        uf[c] = s
        return gch_f * s + u
    s_fin_f = lax.fori_loop(0, nc, scan_f, s0f, unroll=True)

    def scan_b(cc, s):
        c = nc - 1 - cc
        u = ub[c]
        ub[c] = s
        return gch_b * s + u
    s_fin_b = lax.fori_loop(0, nc, scan_b, s0b, unroll=True)

    if emit_state:
        sf_ref[0, 0, h] = s_fin_f[half * 64:half * 64 + 64]
        sb_ref[0, 0, h] = s_fin_b[half * 64:half * 64 + 64]

    gain = gain_ref[0, :, head]

    def emit(c, carry):
        qc = (chunk(q_ref, c, pair).astype(F32) * hmask).astype(BF16)
        kc = (chunk(k_ref, c, pair).astype(F32) * kscale).astype(BF16)
        vc = chunk(v_ref, c, head)
        att = (_dot_nt(qc, kc) * dmat).astype(BF16)
        o = (_dot(att, vc)
             + _dot(qc, uf[c].astype(BF16)) * cross_f
             + _dot(qc, ub[c].astype(BF16)) * cross_b)
        mu = jnp.mean(o, axis=-1, keepdims=True)
        var = jnp.mean(jnp.square(o - mu), axis=-1, keepdims=True)
        on = (o - mu) * lax.rsqrt(var + EPS) * gain
        g = chunk(g_ref, c, head).astype(F32)
        o_ref[0, pl.ds(pl.multiple_of(c * t, t), t), head] = (g * jax.nn.sigmoid(g) * on).astype(BF16)
        return carry
    lax.fori_loop(0, nc, emit, 0, unroll=unroll)


def _retention(z3, ret_decay, ret_norm, l, *, s0f=None, s0b=None, states=None, emit_state=False):
    b, n, _ = z3.shape
    has_state = s0f is not None
    in_specs = [pl.BlockSpec(memory_space=pltpu.SMEM),
                pl.BlockSpec((1, n, 256), lambda bi: (bi, 0, COL_CQ // 2)),
                pl.BlockSpec((1, n, 256), lambda bi: (bi, 0, COL_CK // 2)),
                pl.BlockSpec((1, n, 512), lambda bi: (bi, 0, COL_CV // 4)),
                pl.BlockSpec((1, n, 512), lambda bi: (bi, 0, COL_CG // 4)),
                pl.BlockSpec((1, 1, 512), lambda bi: (l, 0, 0))]
    args = [ret_decay.reshape(-1), z3, z3, z3, z3, ret_norm]
    if has_state:
        in_specs += [pl.BlockSpec((1, 1, 4, 64, 128), lambda bi: (bi, l, 0, 0, 0))] * 2
        args += [s0f, s0b]
    out_specs = [pl.BlockSpec((1, n, 512), lambda bi: (bi, 0, 0))]
    out_shape = [jax.ShapeDtypeStruct((b, n, 512), BF16)]
    aliases = {}
    n_in = len(args)
    if emit_state:
        out_specs += [pl.BlockSpec((1, 1, 4, 64, 128), lambda bi: (bi, l, 0, 0, 0))] * 2
        out_shape += [jax.ShapeDtypeStruct((b, DEPTH, 4, 64, 128), F32)] * 2
        if states is not None:
            in_specs += [pl.BlockSpec(memory_space=pl.ANY)] * 2
            args += list(states)
            aliases = {n_in: 1, n_in + 1: 2}
    kw = dict(n=n, l=l, has_state=has_state, emit_state=emit_state)

    def kern(*refs):
        return _ret_kernel(*refs[:n_in], *refs[len(args):], **kw)
    return pl.pallas_call(
        kern,
        grid=(b,),
        in_specs=in_specs, out_specs=out_specs, out_shape=out_shape,
        scratch_shapes=[pltpu.VMEM((n // RET_CHUNK, 128, 128), F32)] * 2,
        input_output_aliases=aliases,
        compiler_params=_params(("arbitrary",)),
        name="retention_ctx" if emit_state else "retention_lat",
    )(*args)


def _merge_kernel(x_ref, a_ref, b_ref, c_ref, gt_ref, wa_ref, wb_ref, wc_ref, wo_ref,
                  g1_ref, sh_ref, sc_ref, ng_ref, wr_ref, xo_ref, h_ref, lg_ref,
                  wa_s, wb_s, wc_s, wo_s, wr_s):
    @pl.when(pl.program_id(0) == 0)
    def _():
        wa_s[...] = wa_ref[0].astype(BF16)
        wb_s[...] = wb_ref[0].astype(BF16)
        wc_s[...] = wc_ref[0].astype(BF16)
        wo_s[...] = wo_ref[0].astype(BF16)
        wr_s[...] = wr_ref[0].astype(BF16)

    merged = None
    for br, (o_ref, w_s) in enumerate(((a_ref, wa_s), (b_ref, wb_s), (c_ref, wc_s))):
        gate = jax.nn.sigmoid(gt_ref[:, br * D:(br + 1) * D].astype(F32))
        term = gate * _dot(o_ref[...], w_s[...])
        merged = term if merged is None else merged + term
    x = x_ref[...] + g1_ref[0] * _dot(merged.astype(BF16), wo_s[...])
    xo_ref[...] = x
    y = x * lax.rsqrt(jnp.mean(x * x, axis=-1, keepdims=True) + EPS) * ng_ref[0]
    h = (y * (1.0 + sc_ref[0]) + sh_ref[0]).astype(BF16)
    h_ref[...] = h
    lg_ref[...] = _dot_nt(wr_s[...], h)


def _merge(x, a_o, b_o, c_o, gates, mods, norm_ffn, w_pa, w_pb, w_pc, w_out, w_router_t, l, *, tm, row_of_tile):
    t = x.shape[0]
    tok = lambda w: pl.BlockSpec((tm, w), lambda i: (i, 0))
    wspec = lambda r, c: pl.BlockSpec((1, r, c), lambda i: (l, 0, 0))
    return pl.pallas_call(
        _merge_kernel,
        grid=(t // tm,),
        in_specs=[tok(D), tok(512), tok(512), tok(512), tok(G_W),
                  wspec(512, D), wspec(512, D), wspec(512, D), wspec(D, D),
                  _mod_spec(l, 2, row_of_tile, 1), _mod_spec(l, 3, row_of_tile, 1),
                  _mod_spec(l, 4, row_of_tile, 1),
                  pl.BlockSpec((1, 1, D), lambda i: (l, 0, 0)), wspec(N_EXPERTS, D)],
        out_specs=[tok(D), tok(D), pl.BlockSpec((N_EXPERTS, tm), lambda i: (0, i))],
        out_shape=[jax.ShapeDtypeStruct((t, D), F32), jax.ShapeDtypeStruct((t, D), BF16),
                   jax.ShapeDtypeStruct((N_EXPERTS, t), F32)],
        scratch_shapes=[pltpu.VMEM((512, D), BF16)] * 3 + [pltpu.VMEM((D, D), BF16),
                                                            pltpu.VMEM((N_EXPERTS, D), BF16)],
        compiler_params=_params(("arbitrary",)),
        name="merge_out_proj",
    )(x, a_o, b_o, c_o, gates, w_pa, w_pb, w_pc, w_out, mods, mods, mods, norm_ffn, w_router_t)


def _route_kernel(lg_ref, h_ref, xg_ref, gs_ref, pos_ref, aff_s, *, nseg):
    seg_len = TOK_TILE // nseg
    cap = seg_len // 8
    cps = seg_len // 256
    lg = lg_ref[...]
    e = jnp.exp(lg - jnp.max(lg, axis=0, keepdims=True))
    aff = e / jnp.sum(e, axis=0, keepdims=True)
    aff_s[...] = aff

    def seg_sums(x):
        return [jnp.sum(x[:, s * seg_len:(s + 1) * seg_len], axis=1, keepdims=True) for s in range(nseg)]

    def seg_spread(vals):
        return jnp.concatenate([jnp.broadcast_to(v, (N_EXPERTS, seg_len)) for v in vals], axis=1)

    def search(it, thr):
        cand = thr | jnp.left_shift(jnp.int32(1), 30 - it)
        ge = aff >= lax.bitcast_convert_type(cand, F32)
        cnt = seg_spread(seg_sums(jnp.where(ge, 1.0, 0.0)))
        return jnp.where(cnt >= cap, cand, thr)
    thr_bits = lax.fori_loop(0, 31, search, jnp.zeros((N_EXPERTS, TOK_TILE), I32))
    thr = lax.bitcast_convert_type(thr_bits, F32)
    thr_next = lax.bitcast_convert_type(thr_bits + 1, F32)

    r = lax.broadcasted_iota(I32, (256, 256), 0)
    c = lax.broadcasted_iota(I32, (256, 256), 1)
    tri = jnp.where(r < c, 1.0, 0.0).astype(BF16)

    def excl_cumsum(x):
        outs = []
        run = None
        for ch in range(TOK_TILE // 256):
            xc = x[:, ch * 256:(ch + 1) * 256]
            p = _dot(xc.astype(BF16), tri)
            if cps > 1:
                if ch % cps == 0:
                    run = jnp.zeros((N_EXPERTS, 1), F32)
                p = p + run
                run = run + jnp.sum(xc, axis=1, keepdims=True)
            outs.append(p)
        return jnp.concatenate(outs, axis=1)

    gt = jnp.where(aff >= thr_next, 1.0, 0.0)
    eq = jnp.where(aff >= thr, 1.0, 0.0) - gt
    need = cap - seg_spread(seg_sums(gt))
    sel = gt + eq * jnp.where(excl_cumsum(eq) < need, 1.0, 0.0)
    seg_base = seg_spread([jnp.full((N_EXPERTS, 1), float(s * cap), F32) for s in range(nseg)])
    slot = (excl_cumsum(sel) + seg_base).astype(I32)
    pos_ref[...] = jnp.where(sel > 0.5, slot, -1)

    slot_iota = lax.broadcasted_iota(I32, (SLOTS, TOK_TILE), 0)
    h = h_ref[...]
    for grp in range(N_EXPERTS // 4):
        onehots = []
        for ex in range(grp * 4, grp * 4 + 4):
            hit = slot_iota == pos_ref[ex:ex + 1, :]
            gs_ref[ex] = jnp.sum(jnp.where(hit, aff_s[ex:ex + 1, :], 0.0), axis=1, keepdims=True)
            onehots.append(jnp.where(hit, 1.0, 0.0).astype(BF16))
        xg = _dot(jnp.concatenate(onehots, axis=0), h)
        xg_ref[grp * 4:grp * 4 + 4] = xg.astype(BF16).reshape(4, SLOTS, D)


def _route(logits_t, h, *, nseg):
    t = h.shape[0]
    nt = t // TOK_TILE
    return pl.pallas_call(
        functools.partial(_route_kernel, nseg=nseg),
        grid=(nt,),
        in_specs=[pl.BlockSpec((N_EXPERTS, TOK_TILE), lambda i: (0, i)),
                  pl.BlockSpec((TOK_TILE, D), lambda i: (i, 0))],
        out_specs=[pl.BlockSpec((N_EXPERTS, SLOTS, D), lambda i: (0, i, 0)),
                   pl.BlockSpec((N_EXPERTS, SLOTS, 1), lambda i: (0, i, 0)),
                   pl.BlockSpec((N_EXPERTS, TOK_TILE), lambda i: (0, i))],
        out_shape=[jax.ShapeDtypeStruct((N_EXPERTS, nt * SLOTS, D), BF16),
                   jax.ShapeDtypeStruct((N_EXPERTS, nt * SLOTS, 1), F32),
                   jax.ShapeDtypeStruct((N_EXPERTS, t), I32)],
        scratch_shapes=[pltpu.VMEM((N_EXPERTS, TOK_TILE), F32)],
        compiler_params=_params(("arbitrary",)),
        name="route_gather",
    )(logits_t, h)


FFN_TM = 512


def _ffn_kernel(xc_ref, xl_ref, gc_ref, gl_ref, wg_ref, wu_ref, wd_ref, oc_ref, ol_ref, wg_s, wu_s, wd_s):
    m = pl.program_id(1)

    @pl.when(m == 0)
    def _():
        wg_s[...] = wg_ref[0, 0].astype(BF16)
        wu_s[...] = wu_ref[0, 0].astype(BF16)
        wd_s[...] = wd_ref[0, 0].astype(BF16)

    def run(x_ref, g_ref, o_ref):
        x = x_ref[0]
        a = _dot(x, wg_s[...])
        hid = (a * jax.nn.sigmoid(a) * _dot(x, wu_s[...])).astype(BF16)
        o_ref[0] = (_dot(hid, wd_s[...]) * g_ref[0]).astype(BF16)

    @pl.when(m == 0)
    def _():
        run(xc_ref, gc_ref, oc_ref)

    @pl.when(m > 0)
    def _():
        run(xl_ref, gl_ref, ol_ref)


def _expert_ffn(xg_c, xg_l, gs_c, gs_l, w_gate, w_up, w_down, l):
    sc = xg_c.shape[1]
    sl = xg_l.shape[1]
    assert sc == FFN_TM and sl % FFN_TM == 0
    ctx_map = lambda e, m: (e, 0, 0)
    lat_map = lambda e, m: (e, jnp.maximum(m - 1, 0), 0)
    wspec = lambda r, c: pl.BlockSpec((1, 1, r, c), lambda e, m: (l, e, 0, 0))
    return pl.pallas_call(
        _ffn_kernel,
        grid=(N_EXPERTS, 1 + sl // FFN_TM),
        in_specs=[pl.BlockSpec((1, FFN_TM, D), ctx_map), pl.BlockSpec((1, FFN_TM, D), lat_map),
                  pl.BlockSpec((1, FFN_TM, 1), ctx_map), pl.BlockSpec((1, FFN_TM, 1), lat_map),
                  wspec(D, EXPERT_FF), wspec(D, EXPERT_FF), wspec(EXPERT_FF, D)],
        out_specs=[pl.BlockSpec((1, FFN_TM, D), ctx_map), pl.BlockSpec((1, FFN_TM, D), lat_map)],
        out_shape=[jax.ShapeDtypeStruct(xg_c.shape, BF16), jax.ShapeDtypeStruct(xg_l.shape, BF16)],
        scratch_shapes=[pltpu.VMEM((D, EXPERT_FF), BF16), pltpu.VMEM((D, EXPERT_FF), BF16),
                        pltpu.VMEM((EXPERT_FF, D), BF16)],
        compiler_params=_params(("arbitrary",) * 2),
        name="expert_ffn",
    )(xg_c, xg_l, gs_c, gs_l, w_gate, w_up, w_down)


SC_TM = 1024


def _scatter_kernel(x_ref, og_ref, pt_ref, g2_ref, ng_ref, o_ref, *, final):
    lane = lax.broadcasted_iota(I32, (SC_TM, SLOTS), 1)
    pt = pt_ref[...]
    onehot = jnp.concatenate(
        [jnp.where(lane == pt[:, ex:ex + 1], 1.0, 0.0).astype(BF16) for ex in range(N_EXPERTS)], axis=1)
    upd = _dot(onehot, og_ref[...].reshape(N_EXPERTS * SLOTS, D))
    x = x_ref[...] + g2_ref[0] * upd
    if final:
        x = x * lax.rsqrt(jnp.mean(x * x, axis=-1, keepdims=True) + EPS) * ng_ref[...]
    o_ref[...] = x


def _scatter(x, outg, pos_t, mods, norm_final, l, *, row_of_tile, final):
    t = x.shape[0]
    per = TOK_TILE // SC_TM
    return pl.pallas_call(
        functools.partial(_scatter_kernel, final=final),
        grid=(t // SC_TM,),
        in_specs=[pl.BlockSpec((SC_TM, D), lambda i: (i, 0)),
                  pl.BlockSpec((N_EXPERTS, SLOTS, D), lambda i: (0, i // per, 0)),
                  pl.BlockSpec((SC_TM, N_EXPERTS), lambda i: (i, 0)),
                  _mod_spec(l, 5, row_of_tile, 1),
                  pl.BlockSpec((1, D), lambda i: (0, 0))],
        out_specs=pl.BlockSpec((SC_TM, D), lambda i: (i, 0)),
        out_shape=jax.ShapeDtypeStruct((t, D), F32),
        compiler_params=_params(("arbitrary",)),
        name="scatter_residual",
    )(x, outg, pos_t, mods, norm_final)


def kernel(x_prompt, x_sample, cache_a_k, cache_a_v, cache_b_k, cache_b_v, state_c_fwd, state_c_bwd, c, c_ctx, w_mod, b_mod, norm_mix, norm_ffn, w_in, diff_lambda, diff_norm, na_rpb, ret_decay, ret_norm, w_proj_a, w_proj_b, w_proj_c, w_out, w_router, w_exp_gate, w_exp_up, w_exp_down, norm_final):
    nb_c, n_c, _ = x_prompt.shape
    nb_l, n_l, _ = x_sample.shape
    past = cache_a_k.shape[2]
    assert (n_c, n_l, past) == (256, TOK_TILE, 256) and nb_c % 8 == 0 and nb_l + 1 <= 16

    cvec = jnp.zeros((16, D), F32).at[0].set(c_ctx).at[1:1 + nb_l].set(c)
    mods = _modulation(cvec, w_mod, b_mod).reshape(DEPTH * 16, 1, 6 * D)
    tables = _na_tables(na_rpb)
    rope_tabs = _rope_tables(n_l)

    norm_mix3 = norm_mix.reshape(DEPTH, 1, D)
    norm_ffn3 = norm_ffn.reshape(DEPTH, 1, D)
    diff_norm3 = diff_norm.reshape(DEPTH, 1, 512)
    diff_norm_col = diff_norm.reshape(DEPTH, 512, 1)
    ret_norm3 = ret_norm.reshape(DEPTH, 1, 512)
    w_router_t = jnp.swapaxes(w_router, 1, 2)
    norm_final2 = norm_final.reshape(1, D)
    ca_k = cache_a_k.reshape(nb_l, DEPTH, past, 512)
    ca_v = cache_a_v.reshape(nb_l, DEPTH, past, 512)
    cb_k = cache_b_k.reshape(nb_l, DEPTH, past, 512)
    cb_v = cache_b_v.reshape(nb_l, DEPTH, past, 512)

    ctx_row = lambda i: 0
    lat_rows = {tm: (lambda i, per=n_l // tm: 1 + i // per) for tm in (512, SC_TM, TOK_TILE)}

    xc = x_prompt.reshape(nb_c * n_c, D)
    xl = x_sample.reshape(nb_l * n_l, D)
    caches = None
    states = None
    for l in range(DEPTH):
        last = l == DEPTH - 1
        outs = _project(xc, mods, norm_mix3, w_in, l, tm=1024, row_of_tile=ctx_row, caches=caches, n_ctx=nb_c)
        zc, gc, caches = outs[0], outs[1], outs[2:]
        zl, gl = _project(xl, mods, norm_mix3, w_in, l, tm=TOK_TILE, row_of_tile=lat_rows[TOK_TILE],
                          rope_tabs=rope_tabs)
        zc3 = zc.reshape(nb_c, n_c, Z_W)
        zl3 = zl.reshape(nb_l, n_l, Z_W)
        a_c, b_c = _ctx_attention(zc3, diff_lambda, diff_norm3, l)
        a_l = _diff_attention_lat(zl3, ca_k, ca_v, diff_lambda, diff_norm_col, l, tq=512)
        b_l = _na_attention(zl3, cb_k, cb_v, tables, l)
        rc = _retention(zc3, ret_decay, ret_norm3, l, states=states, emit_state=True)
        c_c, states = rc[0], rc[1:]
        c_l = _retention(zl3, ret_decay, ret_norm3, l, s0f=state_c_fwd, s0b=state_c_bwd)[0]
        xc, hc, lgc = _merge(xc, a_c.reshape(-1, 512), b_c.reshape(-1, 512), c_c.reshape(-1, 512), gc, mods,
                             norm_ffn3, w_proj_a, w_proj_b, w_proj_c, w_out, w_router_t, l,
                             tm=512, row_of_tile=ctx_row)
        xl, hl, lgl = _merge(xl, a_l.reshape(-1, 512), b_l.reshape(-1, 512), c_l.reshape(-1, 512), gl, mods,
                             norm_ffn3, w_proj_a, w_proj_b, w_proj_c, w_out, w_router_t, l,
                             tm=512, row_of_tile=lat_rows[512])
        xg_c, gs_c, pos_c = _route(lgc, hc, nseg=TOK_TILE // n_c)
        xg_l, gs_l, pos_l = _route(lgl, hl, nseg=1)
        og_c, og_l = _expert_ffn(xg_c, xg_l, gs_c, gs_l, w_exp_gate, w_exp_up, w_exp_down, l)
        xc = _scatter(xc, og_c, pos_c.T, mods, norm_final2, l, row_of_tile=ctx_row, final=last)
        xl = _scatter(xl, og_l, pos_l.T, mods, norm_final2, l, row_of_tile=lat_rows[SC_TM], final=last)

    new_ak, new_av, new_bk, new_bv = caches
    return (xc.reshape(nb_c, n_c, D), xl.reshape(nb_l, n_l, D),
            new_ak.reshape(nb_c, DEPTH, n_c, 4, 2, A_DH), new_av.reshape(nb_c, DEPTH, n_c, 4, 128),
            new_bk.reshape(nb_c, DEPTH, n_c, 8, 64), new_bv.reshape(nb_c, DEPTH, n_c, 8, 64),
            states[0], states[1])
```

```python
import functools
import math

import numpy as np
import jax
import jax.numpy as jnp
from jax import lax
from jax.experimental import pallas as pl
from jax.experimental.pallas import tpu as pltpu

F32 = jnp.float32
BF16 = jnp.bfloat16
I32 = jnp.int32

D = 1024
DEPTH = 4
GRID_W = 64
A_DH = 64
N_EXPERTS = 16
EXPERT_FF = 1024
RET_CHUNK = 128
ROPE_BASE = 10000.0
EPS = 1e-6
NEG_INF = -1e30

Z_W = 4608
G_W = 3072
PROJ_TN = 512
N_ZT = Z_W // PROJ_TN
N_GT = G_W // PROJ_TN
COL_AQ, COL_AK, COL_AV = 0, 4, 8
COL_BQ, COL_BK, COL_BV = 12, 16, 20
COL_CQ, COL_CK, COL_CV, COL_CG = 24, 26, 28, 32

NA_QR = 4
NA_KR = 12
NA_HALF = 4
NA_ROWS = 8
NA_COLS = 16

TOK_TILE = 2048
SLOTS = 256
VMEM_LIMIT = 56 * 1024 * 1024


def _params(sem):
    return pltpu.CompilerParams(dimension_semantics=sem, vmem_limit_bytes=VMEM_LIMIT)


def _dot(a, b):
    return jnp.dot(a, b, preferred_element_type=F32)


def _dot_nt(a, b):
    return lax.dot_general(a, b, (((1,), (1,)), ((), ())), preferred_element_type=F32)


def _half_masks(dtype, scale=1.0):
    lane = lax.broadcasted_iota(I32, (1, 128), 1)
    lo = jnp.where(lane < 64, scale, 0.0).astype(dtype)
    hi = jnp.where(lane >= 64, scale, 0.0).astype(dtype)
    return lo, hi


def _mod_kernel(c_ref, w_ref, b_ref, o_ref):
    c = c_ref[...]
    s = (c * jax.nn.sigmoid(c)).astype(BF16)
    o_ref[0] = _dot(s, w_ref[0].astype(BF16)) + b_ref[0]


def _modulation(cvec, w_mod, b_mod):
    tn = 1536
    return pl.pallas_call(
        _mod_kernel,
        grid=(DEPTH, 6 * D // tn),
        in_specs=[pl.BlockSpec((16, D), lambda l, j: (0, 0)),
                  pl.BlockSpec((1, D, tn), lambda l, j: (l, 0, j)),
                  pl.BlockSpec((1, 1, tn), lambda l, j: (l, 0, j))],
        out_specs=pl.BlockSpec((1, 16, tn), lambda l, j: (l, 0, j)),
        out_shape=jax.ShapeDtypeStruct((DEPTH, 16, 6 * D), F32),
        compiler_params=_params(("arbitrary", "arbitrary")),
        name="modulation",
    )(cvec, w_mod, b_mod.reshape(DEPTH, 1, 6 * D))


def _mod_spec(l, which, row_of_tile, ngrid):
    if ngrid == 1:
        return pl.BlockSpec((1, 1, D), lambda i: (l * 16 + row_of_tile(i), 0, which))
    return pl.BlockSpec((1, 1, D), lambda i, j: (l * 16 + row_of_tile(i), 0, which))


def _rope_tables(n):
    half = A_DH // 2
    quarter = half // 2
    freqs = ROPE_BASE ** (-np.arange(quarter, dtype=np.float64) * 2.0 / half)
    t = np.arange(n)
    row = (t // GRID_W).astype(np.float64)[:, None] * freqs
    col = (t % GRID_W).astype(np.float64)[:, None] * freqs
    ang = np.concatenate([row, row, col, col], axis=-1)
    ang = np.concatenate([ang, ang], axis=-1)
    first = (np.arange(128) % 32) < 16
    cos = np.cos(ang)
    sin = np.where(first[None, :], -np.sin(ang), np.sin(ang))
    return jnp.asarray(cos, F32), jnp.asarray(sin, F32)


def _proj_kernel(*refs, tm, rope, emit_cache):
    x_ref, sh_ref, sc_ref, g_ref, w_ref = refs[:5]
    k = 5
    if rope:
        cos_ref, sin_ref = refs[k:k + 2]
        k += 2
    z_ref, gt_ref = refs[k:k + 2]
    k += 2
    if emit_cache:
        cache_refs = refs[k:k + 4]
        k += 4
    h_scr = refs[k]
    j = pl.program_id(1)

    @pl.when(j == 0)
    def _():
        x = x_ref[...]
        y = x * lax.rsqrt(jnp.mean(x * x, axis=-1, keepdims=True) + EPS) * g_ref[0]
        h_scr[...] = (y * (1.0 + sc_ref[0]) + sh_ref[0]).astype(BF16)

    z = _dot(h_scr[...], w_ref[0].astype(BF16))

    if rope:
        @pl.when(j < 2)
        def _():
            lane = lax.broadcasted_iota(I32, (tm, 128), 1)
            first = (lane & 31) < 16
            for cc in range(PROJ_TN // 128):
                zc = z[:, cc * 128:(cc + 1) * 128]
                rot = jnp.where(first, pltpu.roll(zc, 112, 1), pltpu.roll(zc, 16, 1))
                z_ref[:, cc * 128:(cc + 1) * 128] = (zc * cos_ref[...] + rot * sin_ref[...]).astype(BF16)

        @pl.when((j >= 2) & (j < N_ZT))
        def _():
            z_ref[...] = z.astype(BF16)
    else:
        @pl.when(j < N_ZT)
        def _():
            z_ref[...] = z.astype(BF16)

    @pl.when(j >= N_ZT)
    def _():
        gt_ref[...] = z.astype(BF16)

    if emit_cache:
        for tile, ref in zip((1, 2, 4, 5), cache_refs):
            @pl.when(j == tile)
            def _(ref=ref):
                ref[...] = z.reshape(tm // 256, 1, 256, PROJ_TN)


def _project(x, mods, norm_g, w_in, l, *, tm, row_of_tile, rope_tabs=None, caches=None, n_ctx=None):
    t = x.shape[0]
    nt = t // tm
    rope = rope_tabs is not None
    emit_cache = n_ctx is not None
    in_specs = [pl.BlockSpec((tm, D), lambda i, j: (i, 0)),
                _mod_spec(l, 0, row_of_tile, 2), _mod_spec(l, 1, row_of_tile, 2),
                pl.BlockSpec((1, 1, D), lambda i, j: (l, 0, 0)),
                pl.BlockSpec((1, D, PROJ_TN), lambda i, j: (l, 0, j))]
    args = [x, mods, mods, norm_g, w_in]
    if rope:
        in_specs += [pl.BlockSpec((tm, 128), lambda i, j: (0, 0))] * 2
        args += list(rope_tabs)
    out_specs = [pl.BlockSpec((tm, PROJ_TN), lambda i, j: (i, jnp.minimum(j, N_ZT - 1))),
                 pl.BlockSpec((tm, PROJ_TN), lambda i, j: (i, jnp.maximum(j - N_ZT, 0)))]
    out_shape = [jax.ShapeDtypeStruct((t, Z_W), BF16), jax.ShapeDtypeStruct((t, G_W), BF16)]
    aliases = {}
    if emit_cache:
        nb = tm // 256
        out_specs += [pl.BlockSpec((nb, 1, 256, PROJ_TN), lambda i, j: (i, l, 0, 0))] * 4
        out_shape += [jax.ShapeDtypeStruct((n_ctx, DEPTH, 256, PROJ_TN), F32)] * 4
        if caches is not None:
            base = len(args)
            in_specs += [pl.BlockSpec(memory_space=pl.ANY)] * 4
            args += list(caches)
            aliases = {base + a: 2 + a for a in range(4)}
    kern = functools.partial(_proj_kernel, tm=tm, rope=rope, emit_cache=emit_cache)
    if emit_cache and caches is not None:
        def kern(*refs):
            n_in = len(args)
            return _proj_kernel(*refs[:n_in - 4], *refs[n_in:], tm=tm, rope=rope, emit_cache=True)
    return pl.pallas_call(
        kern,
        grid=(nt, N_ZT + N_GT),
        in_specs=in_specs, out_specs=out_specs, out_shape=out_shape,
        scratch_shapes=[pltpu.VMEM((tm, D), BF16)],
        input_output_aliases=aliases,
        compiler_params=_params(("arbitrary", "arbitrary")),
        name="in_proj_ctx" if emit_cache else "in_proj_lat",
    )(*args)


ONES_ROWS = 16

def _diff_lambda(dl_ref, lam_init):
    dl = dl_ref[0]
    la = jnp.sum(dl[0:1] * dl[1:2], axis=-1, keepdims=True)
    lb = jnp.sum(dl[2:3] * dl[3:4], axis=-1, keepdims=True)
    return jnp.exp(la) - jnp.exp(lb) + lam_init


def _diff_lat_kernel(q_ref, k_ref, v_ref, ck_ref, cv_ref, dl_ref, gain_ref, o_ref, vt_s, *, n, lam_init):
    @pl.when(pl.program_id(2) == 0)
    def _():
        vt_s[:, :n] = v_ref[0].astype(F32).T.astype(BF16)
        vt_s[:, n:] = cv_ref[0, 0].T.astype(BF16)

    lo, hi = _half_masks(BF16, A_DH ** -0.5)
    kk = k_ref[0]
    ck = ck_ref[0, 0].astype(BF16)
    lam = _diff_lambda(dl_ref, lam_init)
    q = q_ref[0]
    es, dens = [], []
    for msk in (lo, hi):
        qh = q * msk
        s = _dot_nt(kk, qh)
        s2 = _dot_nt(ck, qh)
        m = jnp.maximum(jnp.max(s, axis=0, keepdims=True), jnp.max(s2, axis=0, keepdims=True))
        e = jnp.exp(s - m)
        e2 = jnp.exp(s2 - m)
        es.append((e, e2))
        dens.append(jnp.sum(e, axis=0, keepdims=True) + jnp.sum(e2, axis=0, keepdims=True))
    r = lam * dens[0] / dens[1]
    a = (es[0][0] - r * es[1][0]).astype(BF16)
    a2 = (es[0][1] - r * es[1][1]).astype(BF16)
    ot = (_dot(vt_s[:, :n], a) + _dot(vt_s[:, n:], a2)) / dens[0]
    on = ot * lax.rsqrt(jnp.mean(ot * ot, axis=0, keepdims=True) + EPS)
    o_ref[0] = ((on * gain_ref[0]) * (1.0 - lam_init)).T.astype(BF16)


def _diff_attention_lat(z3, cache_k, cache_v, diff_lambda, diff_norm_col, l, *, tq):
    b, n, _ = z3.shape
    past = cache_k.shape[2]
    lam_init = 0.8 - 0.6 * math.exp(-0.3 * l)
    return pl.pallas_call(
        functools.partial(_diff_lat_kernel, n=n, lam_init=lam_init),
        grid=(b, 4, n // tq),
        in_specs=[pl.BlockSpec((1, tq, 128), lambda bi, h, qi: (bi, qi, COL_AQ + h)),
                  pl.BlockSpec((1, n, 128), lambda bi, h, qi: (bi, 0, COL_AK + h)),
                  pl.BlockSpec((1, n, 128), lambda bi, h, qi: (bi, 0, COL_AV + h)),
                  pl.BlockSpec((1, 1, past, 128), lambda bi, h, qi: (bi, l, 0, h)),
                  pl.BlockSpec((1, 1, past, 128), lambda bi, h, qi: (bi, l, 0, h)),
                  pl.BlockSpec((1, 4, A_DH), lambda bi, h, qi: (l, 0, 0)),
                  pl.BlockSpec((1, 128, 1), lambda bi, h, qi: (l, h, 0))],
        out_specs=pl.BlockSpec((1, tq, 128), lambda bi, h, qi: (bi, qi, h)),
        out_shape=jax.ShapeDtypeStruct((b, n, 512), BF16),
        scratch_shapes=[pltpu.VMEM((128, n + past), BF16)],
        compiler_params=_params(("arbitrary",) * 3),
        name="diff_attn_lat",
    )(z3, z3, z3, cache_k, cache_v, diff_lambda, diff_norm_col)


def _ctx_attn_kernel(aq_ref, ak_ref, av_ref, bq_ref, bk_ref, bv_ref, dl_ref, gain_ref, ao_ref, bo_ref,
                     *, lam_init):
    lo, hi = _half_masks(BF16, A_DH ** -0.5)
    one_lo, one_hi = _half_masks(BF16)
    lam = _diff_lambda(dl_ref, lam_init)
    for h in range(4):
        hs = slice(h * 128, h * 128 + 128)
        q = aq_ref[0, :, hs]
        kk = ak_ref[0, :, hs]
        v = av_ref[0, :, hs]
        outs = []
        for msk in (lo, hi):
            s = _dot_nt(q * msk, kk)
            e = jnp.exp(s - jnp.max(s, axis=-1, keepdims=True))
            outs.append(_dot(e.astype(BF16), v) / jnp.sum(e, axis=-1, keepdims=True))
        o = outs[0] - lam * outs[1]
        on = o * lax.rsqrt(jnp.mean(o * o, axis=-1, keepdims=True) + EPS)
        ao_ref[0, :, hs] = ((on * gain_ref[0, :, hs]) * (1.0 - lam_init)).astype(BF16)

        q = bq_ref[0, :, hs]
        kk = bk_ref[0, :, hs]
        v = bv_ref[0, :, hs]
        acc = None
        for msk, vmsk in ((lo, one_lo), (hi, one_hi)):
            s = _dot_nt(q * msk, kk)
            e = jnp.exp(s - jnp.max(s, axis=-1, keepdims=True))
            o = _dot(e.astype(BF16), v * vmsk) / jnp.sum(e, axis=-1, keepdims=True)
            acc = o if acc is None else acc + o
        bo_ref[0, :, hs] = acc.astype(BF16)


def _ctx_attention(z3, diff_lambda, diff_norm, l):
    b, n, _ = z3.shape
    lam_init = 0.8 - 0.6 * math.exp(-0.3 * l)
    col = lambda c: pl.BlockSpec((1, n, 512), lambda bi: (bi, 0, c))
    return pl.pallas_call(
        functools.partial(_ctx_attn_kernel, lam_init=lam_init),
        grid=(b,),
        in_specs=[col(0), col(1), col(2), col(3), col(4), col(5),
                  pl.BlockSpec((1, 4, A_DH), lambda bi: (l, 0, 0)),
                  pl.BlockSpec((1, 1, 512), lambda bi: (l, 0, 0))],
        out_specs=[pl.BlockSpec((1, n, 512), lambda bi: (bi, 0, 0))] * 2,
        out_shape=[jax.ShapeDtypeStruct((b, n, 512), BF16)] * 2,
        compiler_params=_params(("arbitrary",)),
        name="attn_ctx",
    )(z3, z3, z3, z3, z3, z3, diff_lambda, diff_norm)


def _na_table_kernel(rpb_ref, o_ref):
    l = pl.program_id(0)
    h = pl.program_id(1)
    nrel_r = 2 * NA_ROWS - 1
    nrel_c = 2 * NA_COLS - 1
    base = (l * 8 + h) * (nrel_r * nrel_c)
    kc = lax.broadcasted_iota(I32, (GRID_W, GRID_W), 0)
    qc = lax.broadcasted_iota(I32, (GRID_W, GRID_W), 1)
    dcm = jnp.clip(kc - qc + (NA_COLS - 1), 0, nrel_c - 1)
    ws = jnp.clip(qc - NA_COLS // 2, 0, GRID_W - NA_COLS)
    col_ok = (kc >= ws) & (kc < ws + NA_COLS)
    neg = jnp.full((GRID_W, GRID_W), NEG_INF, F32)
    tabs = []
    for dr in range(nrel_r):
        acc = jnp.zeros((GRID_W, GRID_W), F32)
        for dc in range(nrel_c):
            acc = jnp.where(dcm == dc, rpb_ref[base + dr * nrel_c + dc], acc)
        tabs.append(jnp.where(col_ok, acc, neg))
    for typ in range(3):
        for i in range(NA_QR):
            for j in range(NA_KR):
                if typ == 0:
                    valid, dr = j < NA_ROWS, j - i + 7
                elif typ == 1:
                    valid, dr = i <= j < i + NA_ROWS, j - i + 3
                else:
                    valid, dr = j >= 4, j - i - 1
                o_ref[0, 0, typ, j * 64:(j + 1) * 64, i * 64:(i + 1) * 64] = tabs[dr] if valid else neg


def _na_tables(na_rpb):
    return pl.pallas_call(
        _na_table_kernel,
        grid=(DEPTH, 8),
        in_specs=[pl.BlockSpec(memory_space=pltpu.SMEM)],
        out_specs=pl.BlockSpec((1, 1, 3, NA_KR * 64, NA_QR * 64), lambda l, h: (l, h, 0, 0, 0)),
        out_shape=jax.ShapeDtypeStruct((DEPTH, 8, 3, NA_KR * 64, NA_QR * 64), F32),
        compiler_params=_params(("arbitrary",) * 2),
        name="na_bias_tables",
    )(na_rpb.reshape(-1))


def _na_kernel(q_ref, k_ref, v_ref, ck_ref, cv_ref, tab_ref, o_ref, vt_s, cvt_s, *, n):
    rb = pl.program_id(2)
    nrb = pl.num_programs(2)

    @pl.when(rb == 0)
    def _():
        for pr in range(2):
            ps = slice(pr * 128, pr * 128 + 128)
            vt = v_ref[0, :, ps].astype(F32).T
            for c in range(n // 128):
                vt_s[pr, c] = vt[:, c * 128:(c + 1) * 128].astype(BF16)
            cvt_s[pr] = cv_ref[0, 0, :, ps].T.astype(BF16)

    start = jnp.clip(rb * NA_QR - NA_HALF, 0, 32 - NA_KR)
    typ = jnp.where(rb == 0, 0, jnp.where(rb == nrb - 1, 2, 1))
    off = pl.multiple_of(start * GRID_W, 256)
    c0 = start // 2
    nchunk = NA_KR * GRID_W // 128
    lo, hi = _half_masks(BF16, 0.125)
    for pr in range(2):
        ps = slice(pr * 128, pr * 128 + 128)
        q = q_ref[0, :, ps]
        kw = k_ref[0, pl.ds(off, NA_KR * GRID_W), ps]
        ck = ck_ref[0, 0, :, ps].astype(BF16)
        vtw = jnp.concatenate([vt_s[pr, c0 + c] for c in range(nchunk)], axis=1)
        cvt = cvt_s[pr]
        outs = []
        for half, msk in enumerate((lo, hi)):
            qh = q * msk
            s = _dot_nt(kw, qh) + tab_ref[0, pr * 2 + half, typ]
            s2 = _dot_nt(ck, qh)
            m = jnp.maximum(jnp.max(s, axis=0, keepdims=True), jnp.max(s2, axis=0, keepdims=True))
            e = jnp.exp(s - m).astype(BF16)
            e2 = jnp.exp(s2 - m).astype(BF16)
            hr = slice(half * 64, half * 64 + 64)
            acc = (_dot(jnp.concatenate([vtw[hr], jnp.ones((ONES_ROWS, vtw.shape[1]), BF16)], axis=0), e)
                   + _dot(jnp.concatenate([cvt[hr], jnp.ones((ONES_ROWS, cvt.shape[1]), BF16)], axis=0), e2))
            outs.append(acc[:64] / acc[64:65])
        o_ref[0, :, ps] = jnp.concatenate(outs, axis=0).T.astype(BF16)


def _na_attention(z3, cache_k, cache_v, tables, l):
    b, n, _ = z3.shape
    past = cache_k.shape[2]
    tq = NA_QR * GRID_W
    return pl.pallas_call(
        functools.partial(_na_kernel, n=n),
        grid=(2, b, n // tq),
        in_specs=[pl.BlockSpec((1, tq, 256), lambda pp, bi, rb: (bi, rb, COL_BQ // 2 + pp)),
                  pl.BlockSpec((1, n, 256), lambda pp, bi, rb: (bi, 0, COL_BK // 2 + pp)),
                  pl.BlockSpec((1, n, 256), lambda pp, bi, rb: (bi, 0, COL_BV // 2 + pp)),
                  pl.BlockSpec((1, 1, past, 256), lambda pp, bi, rb: (bi, l, 0, pp)),
                  pl.BlockSpec((1, 1, past, 256), lambda pp, bi, rb: (bi, l, 0, pp)),
                  pl.BlockSpec((1, 4, 3, NA_KR * 64, tq), lambda pp, bi, rb: (l, pp, 0, 0, 0))],
        out_specs=pl.BlockSpec((1, tq, 256), lambda pp, bi, rb: (bi, rb, pp)),
        out_shape=jax.ShapeDtypeStruct((b, n, 512), BF16),
        scratch_shapes=[pltpu.VMEM((2, n // 128, 128, 128), BF16), pltpu.VMEM((2, 128, past), BF16)],
        compiler_params=_params(("arbitrary",) * 3),
        name="na_attn_lat",
    )(z3, z3, z3, cache_k, cache_v, tables)


RET_UNROLL = 4


def _ret_kernel(*refs, **kw):
    for h in range(4):
        _ret_head(*refs, h=h, **kw)


def _ret_head(*refs, h, n, l, has_state, emit_state):
    rd_ref, q_ref, k_ref, v_ref, g_ref, gain_ref = refs[:6]
    i = 6
    if has_state:
        s0f_ref, s0b_ref = refs[i:i + 2]
        i += 2
    o_ref = refs[i]
    i += 1
    if emit_state:
        sf_ref, sb_ref = refs[i:i + 2]
        i += 2
    uf, ub = refs[i:i + 2]
    t = RET_CHUNK
    nc = n // t
    unroll = min(nc, RET_UNROLL)
    half = h % 2
    pair = slice((h // 2) * 128, (h // 2) * 128 + 128)
    head = slice(h * 128, h * 128 + 128)

    def log_gamma(direction):
        x = jnp.full((1, 1), rd_ref[l * 8 + direction * 4 + h], F32)
        return jnp.minimum(x, 0.0) - jnp.log1p(jnp.exp(-jnp.abs(x)))

    lgf = log_gamma(0)
    lgb = log_gamma(1)
    lane = lax.broadcasted_iota(I32, (1, 128), 1)
    hmask = jnp.where((lane >= 64) if half else (lane < 64), 1.0, 0.0)
    ri = lax.broadcasted_iota(I32, (t, t), 0).astype(F32)
    ci = lax.broadcasted_iota(I32, (t, t), 1).astype(F32)
    diff = ri - ci
    dmat = jnp.where(diff > 0, jnp.exp(lgf * jnp.maximum(diff, 0.0)),
                     jnp.where(diff < 0, jnp.exp(lgb * jnp.maximum(-diff, 0.0)), 2.0))
    pos_r = lax.broadcasted_iota(I32, (1, t), 1).astype(F32)
    pos_c = lax.broadcasted_iota(I32, (t, 1), 0).astype(F32)
    upd_f = jnp.exp(lgf * (t - 1.0 - pos_r))
    upd_b = jnp.exp(lgb * pos_r)
    cross_f = jnp.exp(lgf * (pos_c + 1.0))
    cross_b = jnp.exp(lgb * (t - pos_c))
    gch_f = jnp.exp(lgf * t)
    gch_b = jnp.exp(lgb * t)
    kscale = hmask * (64 ** -0.5)

    def chunk(ref, c, lanes):
        return ref[0, pl.ds(pl.multiple_of(c * t, t), t), lanes]

    def summarise(c, carry):
        kt = (chunk(k_ref, c, pair).astype(F32) * kscale).T
        vc = chunk(v_ref, c, head)
        uf[c] = _dot((kt * upd_f).astype(BF16), vc)
        ub[c] = _dot((kt * upd_b).astype(BF16), vc)
        return carry
    lax.fori_loop(0, nc, summarise, 0, unroll=unroll)

    if has_state:
        s0f = jnp.concatenate([s0f_ref[0, 0, h], s0f_ref[0, 0, h]], axis=0)
        s0b = jnp.concatenate([s0b_ref[0, 0, h], s0b_ref[0, 0, h]], axis=0)
    else:
        s0f = jnp.zeros((128, 128), F32)
        s0b = s0f

    def scan_f(c, s):
        u = uf[c]
        uf[c] = s
        return gch_f * s + u
    s_fin_f = lax.fori_loop(0, nc, scan_f, s0f, unroll=True)

    def scan_b(cc, s):
        c = nc - 1 - cc
        u = ub[c]
        ub[c] = s
        return gch_b * s + u
    s_fin_b = lax.fori_loop(0, nc, scan_b, s0b, unroll=True)

    if emit_state:
        sf_ref[0, 0, h] = s_fin_f[half * 64:half * 64 + 64]
        sb_ref[0, 0, h] = s_fin_b[half * 64:half * 64 + 64]

    gain = gain_ref[0, :, head]

    def emit(c, carry):
        qc = (chunk(q_ref, c, pair).astype(F32) * hmask).astype(BF16)
        kc = (chunk(k_ref, c, pair).astype(F32) * kscale).astype(BF16)
        vc = chunk(v_ref, c, head)
        att = (_dot_nt(qc, kc) * dmat).astype(BF16)
        o = (_dot(att, vc)
             + _dot(qc, uf[c].astype(BF16)) * cross_f
             + _dot(qc, ub[c].astype(BF16)) * cross_b)
        mu = jnp.mean(o, axis=-1, keepdims=True)
        var = jnp.mean(jnp.square(o - mu), axis=-1, keepdims=True)
        on = (o - mu) * lax.rsqrt(var + EPS) * gain
        g = chunk(g_ref, c, head).astype(F32)
        o_ref[0, pl.ds(pl.multiple_of(c * t, t), t), head] = (g * jax.nn.sigmoid(g) * on).astype(BF16)
        return carry
    lax.fori_loop(0, nc, emit, 0, unroll=unroll)


def _retention(z3, ret_decay, ret_norm, l, *, s0f=None, s0b=None, states=None, emit_state=False):
    b, n, _ = z3.shape
    has_state = s0f is not None
    in_specs = [pl.BlockSpec(memory_space=pltpu.SMEM),
                pl.BlockSpec((1, n, 256), lambda bi: (bi, 0, COL_CQ // 2)),
                pl.BlockSpec((1, n, 256), lambda bi: (bi, 0, COL_CK // 2)),
                pl.BlockSpec((1, n, 512), lambda bi: (bi, 0, COL_CV // 4)),
                pl.BlockSpec((1, n, 512), lambda bi: (bi, 0, COL_CG // 4)),
                pl.BlockSpec((1, 1, 512), lambda bi: (l, 0, 0))]
    args = [ret_decay.reshape(-1), z3, z3, z3, z3, ret_norm]
    if has_state:
        in_specs += [pl.BlockSpec((1, 1, 4, 64, 128), lambda bi: (bi, l, 0, 0, 0))] * 2
        args += [s0f, s0b]
    out_specs = [pl.BlockSpec((1, n, 512), lambda bi: (bi, 0, 0))]
    out_shape = [jax.ShapeDtypeStruct((b, n, 512), BF16)]
    aliases = {}
    n_in = len(args)
    if emit_state:
        out_specs += [pl.BlockSpec((1, 1, 4, 64, 128), lambda bi: (bi, l, 0, 0, 0))] * 2
        out_shape += [jax.ShapeDtypeStruct((b, DEPTH, 4, 64, 128), F32)] * 2
        if states is not None:
            in_specs += [pl.BlockSpec(memory_space=pl.ANY)] * 2
            args += list(states)
            aliases = {n_in: 1, n_in + 1: 2}
    kw = dict(n=n, l=l, has_state=has_state, emit_state=emit_state)

    def kern(*refs):
        return _ret_kernel(*refs[:n_in], *refs[len(args):], **kw)
    return pl.pallas_call(
        kern,
        grid=(b,),
        in_specs=in_specs, out_specs=out_specs, out_shape=out_shape,
        scratch_shapes=[pltpu.VMEM((n // RET_CHUNK, 128, 128), F32)] * 2,
        input_output_aliases=aliases,
        compiler_params=_params(("arbitrary",)),
        name="retention_ctx" if emit_state else "retention_lat",
    )(*args)


def _merge_kernel(x_ref, a_ref, b_ref, c_ref, gt_ref, wa_ref, wb_ref, wc_ref, wo_ref,
                  g1_ref, sh_ref, sc_ref, ng_ref, wr_ref, xo_ref, h_ref, lg_ref,
                  wa_s, wb_s, wc_s, wo_s, wr_s):
    @pl.when(pl.program_id(0) == 0)
    def _():
        wa_s[...] = wa_ref[0].astype(BF16)
        wb_s[...] = wb_ref[0].astype(BF16)
        wc_s[...] = wc_ref[0].astype(BF16)
        wo_s[...] = wo_ref[0].astype(BF16)
        wr_s[...] = wr_ref[0].astype(BF16)

    merged = None
    for br, (o_ref, w_s) in enumerate(((a_ref, wa_s), (b_ref, wb_s), (c_ref, wc_s))):
        gate = jax.nn.sigmoid(gt_ref[:, br * D:(br + 1) * D].astype(F32))
        term = gate * _dot(o_ref[...], w_s[...])
        merged = term if merged is None else merged + term
    x = x_ref[...] + g1_ref[0] * _dot(merged.astype(BF16), wo_s[...])
    xo_ref[...] = x
    y = x * lax.rsqrt(jnp.mean(x * x, axis=-1, keepdims=True) + EPS) * ng_ref[0]
    h = (y * (1.0 + sc_ref[0]) + sh_ref[0]).astype(BF16)
    h_ref[...] = h
    lg_ref[...] = _dot_nt(wr_s[...], h)


def _merge(x, a_o, b_o, c_o, gates, mods, norm_ffn, w_pa, w_pb, w_pc, w_out, w_router_t, l, *, tm, row_of_tile):
    t = x.shape[0]
    tok = lambda w: pl.BlockSpec((tm, w), lambda i: (i, 0))
    wspec = lambda r, c: pl.BlockSpec((1, r, c), lambda i: (l, 0, 0))
    return pl.pallas_call(
        _merge_kernel,
        grid=(t // tm,),
        in_specs=[tok(D), tok(512), tok(512), tok(512), tok(G_W),
                  wspec(512, D), wspec(512, D), wspec(512, D), wspec(D, D),
                  _mod_spec(l, 2, row_of_tile, 1), _mod_spec(l, 3, row_of_tile, 1),
                  _mod_spec(l, 4, row_of_tile, 1),
                  pl.BlockSpec((1, 1, D), lambda i: (l, 0, 0)), wspec(N_EXPERTS, D)],
        out_specs=[tok(D), tok(D), pl.BlockSpec((N_EXPERTS, tm), lambda i: (0, i))],
        out_shape=[jax.ShapeDtypeStruct((t, D), F32), jax.ShapeDtypeStruct((t, D), BF16),
                   jax.ShapeDtypeStruct((N_EXPERTS, t), F32)],
        scratch_shapes=[pltpu.VMEM((512, D), BF16)] * 3 + [pltpu.VMEM((D, D), BF16),
                                                            pltpu.VMEM((N_EXPERTS, D), BF16)],
        compiler_params=_params(("arbitrary",)),
        name="merge_out_proj",
    )(x, a_o, b_o, c_o, gates, w_pa, w_pb, w_pc, w_out, mods, mods, mods, norm_ffn, w_router_t)


def _route_kernel(lg_ref, h_ref, xg_ref, gs_ref, pos_ref, aff_s, *, nseg):
    seg_len = TOK_TILE // nseg
    cap = seg_len // 8
    cps = seg_len // 256
    lg = lg_ref[...]
    e = jnp.exp(lg - jnp.max(lg, axis=0, keepdims=True))
    aff = e / jnp.sum(e, axis=0, keepdims=True)
    aff_s[...] = aff

    def seg_sums(x):
        return [jnp.sum(x[:, s * seg_len:(s + 1) * seg_len], axis=1, keepdims=True) for s in range(nseg)]

    def seg_spread(vals):
        return jnp.concatenate([jnp.broadcast_to(v, (N_EXPERTS, seg_len)) for v in vals], axis=1)

    def search(it, thr):
        cand = thr | jnp.left_shift(jnp.int32(1), 30 - it)
        ge = aff >= lax.bitcast_convert_type(cand, F32)
        cnt = seg_spread(seg_sums(jnp.where(ge, 1.0, 0.0)))
        return jnp.where(cnt >= cap, cand, thr)
    thr_bits = lax.fori_loop(0, 31, search, jnp.zeros((N_EXPERTS, TOK_TILE), I32))
    thr = lax.bitcast_convert_type(thr_bits, F32)
    thr_next = lax.bitcast_convert_type(thr_bits + 1, F32)

    r = lax.broadcasted_iota(I32, (256, 256), 0)
    c = lax.broadcasted_iota(I32, (256, 256), 1)
    tri = jnp.where(r < c, 1.0, 0.0).astype(BF16)

    def excl_cumsum(x):
        outs = []
        run = None
        for ch in range(TOK_TILE // 256):
            xc = x[:, ch * 256:(ch + 1) * 256]
            p = _dot(xc.astype(BF16), tri)
            if cps > 1:
                if ch % cps == 0:
                    run = jnp.zeros((N_EXPERTS, 1), F32)
                p = p + run
                run = run + jnp.sum(xc, axis=1, keepdims=True)
            outs.append(p)
        return jnp.concatenate(outs, axis=1)

    gt = jnp.where(aff >= thr_next, 1.0, 0.0)
    eq = jnp.where(aff >= thr, 1.0, 0.0) - gt
    need = cap - seg_spread(seg_sums(gt))
    sel = gt + eq * jnp.where(excl_cumsum(eq) < need, 1.0, 0.0)
    seg_base = seg_spread([jnp.full((N_EXPERTS, 1), float(s * cap), F32) for s in range(nseg)])
    slot = (excl_cumsum(sel) + seg_base).astype(I32)
    pos_ref[...] = jnp.where(sel > 0.5, slot, -1)

    slot_iota = lax.broadcasted_iota(I32, (SLOTS, TOK_TILE), 0)
    h = h_ref[...]
    for grp in range(N_EXPERTS // 4):
        onehots = []
        for ex in range(grp * 4, grp * 4 + 4):
            hit = slot_iota == pos_ref[ex:ex + 1, :]
            gs_ref[ex] = jnp.sum(jnp.where(hit, aff_s[ex:ex + 1, :], 0.0), axis=1, keepdims=True)
            onehots.append(jnp.where(hit, 1.0, 0.0).astype(BF16))
        xg = _dot(jnp.concatenate(onehots, axis=0), h)
        xg_ref[grp * 4:grp * 4 + 4] = xg.astype(BF16).reshape(4, SLOTS, D)


def _route(logits_t, h, *, nseg):
    t = h.shape[0]
    nt = t // TOK_TILE
    return pl.pallas_call(
        functools.partial(_route_kernel, nseg=nseg),
        grid=(nt,),
        in_specs=[pl.BlockSpec((N_EXPERTS, TOK_TILE), lambda i: (0, i)),
                  pl.BlockSpec((TOK_TILE, D), lambda i: (i, 0))],
        out_specs=[pl.BlockSpec((N_EXPERTS, SLOTS, D), lambda i: (0, i, 0)),
                   pl.BlockSpec((N_EXPERTS, SLOTS, 1), lambda i: (0, i, 0)),
                   pl.BlockSpec((N_EXPERTS, TOK_TILE), lambda i: (0, i))],
        out_shape=[jax.ShapeDtypeStruct((N_EXPERTS, nt * SLOTS, D), BF16),
                   jax.ShapeDtypeStruct((N_EXPERTS, nt * SLOTS, 1), F32),
                   jax.ShapeDtypeStruct((N_EXPERTS, t), I32)],
        scratch_shapes=[pltpu.VMEM((N_EXPERTS, TOK_TILE), F32)],
        compiler_params=_params(("arbitrary",)),
        name="route_gather",
    )(logits_t, h)


FFN_SPLIT = 2
FFN_CHUNK = 512


def _ffn_kernel(xc_ref, xl_ref, gc_ref, gl_ref, wg_ref, wu_ref, wd_ref, oc_ref, ol_ref, wg_s, wu_s, wd_s):
    @pl.when(pl.program_id(1) == 0)
    def _():
        wg_s[...] = wg_ref[0, 0].astype(BF16)
        wu_s[...] = wu_ref[0, 0].astype(BF16)
        wd_s[...] = wd_ref[0, 0].astype(BF16)

    for x_ref, g_ref, o_ref in ((xc_ref, gc_ref, oc_ref), (xl_ref, gl_ref, ol_ref)):
        rows = x_ref.shape[1]
        step = min(rows, FFN_CHUNK)
        for r0 in range(0, rows, step):
            rs = slice(r0, r0 + step)
            x = x_ref[0, rs, :]
            a = _dot(x, wg_s[...])
            hid = (a * jax.nn.sigmoid(a) * _dot(x, wu_s[...])).astype(BF16)
            o_ref[0, rs, :] = (_dot(hid, wd_s[...]) * g_ref[0, rs, :]).astype(BF16)


def _expert_ffn(xg_c, xg_l, gs_c, gs_l, w_gate, w_up, w_down, l):
    rc = xg_c.shape[1] // FFN_SPLIT
    rl = xg_l.shape[1] // FFN_SPLIT
    assert rc % 16 == 0 and rl % min(rl, FFN_CHUNK) == 0
    row_map = lambda e, m: (e, m, 0)
    wspec = lambda r, c: pl.BlockSpec((1, 1, r, c), lambda e, m: (l, e, 0, 0))
    return pl.pallas_call(
        _ffn_kernel,
        grid=(N_EXPERTS, FFN_SPLIT),
        in_specs=[pl.BlockSpec((1, rc, D), row_map), pl.BlockSpec((1, rl, D), row_map),
                  pl.BlockSpec((1, rc, 1), row_map), pl.BlockSpec((1, rl, 1), row_map),
                  wspec(D, EXPERT_FF), wspec(D, EXPERT_FF), wspec(EXPERT_FF, D)],
        out_specs=[pl.BlockSpec((1, rc, D), row_map), pl.BlockSpec((1, rl, D), row_map)],
        out_shape=[jax.ShapeDtypeStruct(xg_c.shape, BF16), jax.ShapeDtypeStruct(xg_l.shape, BF16)],
        scratch_shapes=[pltpu.VMEM((D, EXPERT_FF), BF16), pltpu.VMEM((D, EXPERT_FF), BF16),
                        pltpu.VMEM((EXPERT_FF, D), BF16)],
        compiler_params=_params(("arbitrary",) * 2),
        name="expert_ffn",
    )(xg_c, xg_l, gs_c, gs_l, w_gate, w_up, w_down)


SC_TM = 1024


def _scatter_kernel(x_ref, og_ref, pt_ref, g2_ref, ng_ref, o_ref, *, final):
    lane = lax.broadcasted_iota(I32, (SC_TM, SLOTS), 1)
    pt = pt_ref[...]
    onehot = jnp.concatenate(
        [jnp.where(lane == pt[:, ex:ex + 1], 1.0, 0.0).astype(BF16) for ex in range(N_EXPERTS)], axis=1)
    upd = _dot(onehot, og_ref[...].reshape(N_EXPERTS * SLOTS, D))
    x = x_ref[...] + g2_ref[0] * upd
    if final:
        x = x * lax.rsqrt(jnp.mean(x * x, axis=-1, keepdims=True) + EPS) * ng_ref[...]
    o_ref[...] = x


def _scatter(x, outg, pos_t, mods, norm_final, l, *, row_of_tile, final):
    t = x.shape[0]
    per = TOK_TILE // SC_TM
    return pl.pallas_call(
        functools.partial(_scatter_kernel, final=final),
        grid=(t // SC_TM,),
        in_specs=[pl.BlockSpec((SC_TM, D), lambda i: (i, 0)),
                  pl.BlockSpec((N_EXPERTS, SLOTS, D), lambda i: (0, i // per, 0)),
                  pl.BlockSpec((SC_TM, N_EXPERTS), lambda i: (i, 0)),
                  _mod_spec(l, 5, row_of_tile, 1),
                  pl.BlockSpec((1, D), lambda i: (0, 0))],
        out_specs=pl.BlockSpec((SC_TM, D), lambda i: (i, 0)),
        out_shape=jax.ShapeDtypeStruct((t, D), F32),
        compiler_params=_params(("arbitrary",)),
        name="scatter_residual",
    )(x, outg, pos_t, mods, norm_final)


def kernel(x_prompt, x_sample, cache_a_k, cache_a_v, cache_b_k, cache_b_v, state_c_fwd, state_c_bwd, c, c_ctx, w_mod, b_mod, norm_mix, norm_ffn, w_in, diff_lambda, diff_norm, na_rpb, ret_decay, ret_norm, w_proj_a, w_proj_b, w_proj_c, w_out, w_router, w_exp_gate, w_exp_up, w_exp_down, norm_final):
    nb_c, n_c, _ = x_prompt.shape
    nb_l, n_l, _ = x_sample.shape
    past = cache_a_k.shape[2]
    assert (n_c, n_l, past) == (256, TOK_TILE, 256) and nb_c % 8 == 0 and nb_l + 1 <= 16

    cvec = jnp.zeros((16, D), F32).at[0].set(c_ctx).at[1:1 + nb_l].set(c)
    mods = _modulation(cvec, w_mod, b_mod).reshape(DEPTH * 16, 1, 6 * D)
    tables = _na_tables(na_rpb)
    rope_tabs = _rope_tables(n_l)

    norm_mix3 = norm_mix.reshape(DEPTH, 1, D)
    norm_ffn3 = norm_ffn.reshape(DEPTH, 1, D)
    diff_norm3 = diff_norm.reshape(DEPTH, 1, 512)
    diff_norm_col = diff_norm.reshape(DEPTH, 512, 1)
    ret_norm3 = ret_norm.reshape(DEPTH, 1, 512)
    w_router_t = jnp.swapaxes(w_router, 1, 2)
    norm_final2 = norm_final.reshape(1, D)
    ca_k = cache_a_k.reshape(nb_l, DEPTH, past, 512)
    ca_v = cache_a_v.reshape(nb_l, DEPTH, past, 512)
    cb_k = cache_b_k.reshape(nb_l, DEPTH, past, 512)
    cb_v = cache_b_v.reshape(nb_l, DEPTH, past, 512)

    ctx_row = lambda i: 0
    lat_rows = {tm: (lambda i, per=n_l // tm: 1 + i // per) for tm in (512, SC_TM, TOK_TILE)}

    xc = x_prompt.reshape(nb_c * n_c, D)
    xl = x_sample.reshape(nb_l * n_l, D)
    caches = None
    states = None
    for l in range(DEPTH):
        last = l == DEPTH - 1
        outs = _project(xc, mods, norm_mix3, w_in, l, tm=1024, row_of_tile=ctx_row, caches=caches, n_ctx=nb_c)
        zc, gc, caches = outs[0], outs[1], outs[2:]
        zl, gl = _project(xl, mods, norm_mix3, w_in, l, tm=TOK_TILE, row_of_tile=lat_rows[TOK_TILE],
                          rope_tabs=rope_tabs)
        zc3 = zc.reshape(nb_c, n_c, Z_W)
        zl3 = zl.reshape(nb_l, n_l, Z_W)
        a_c, b_c = _ctx_attention(zc3, diff_lambda, diff_norm3, l)
        a_l = _diff_attention_lat(zl3, ca_k, ca_v, diff_lambda, diff_norm_col, l, tq=512)
        b_l = _na_attention(zl3, cb_k, cb_v, tables, l)
        rc = _retention(zc3, ret_decay, ret_norm3, l, states=states, emit_state=True)
        c_c, states = rc[0], rc[1:]
        c_l = _retention(zl3, ret_decay, ret_norm3, l, s0f=state_c_fwd, s0b=state_c_bwd)[0]
        xc, hc, lgc = _merge(xc, a_c.reshape(-1, 512), b_c.reshape(-1, 512), c_c.reshape(-1, 512), gc, mods,
                             norm_ffn3, w_proj_a, w_proj_b, w_proj_c, w_out, w_router_t, l,
                             tm=512, row_of_tile=ctx_row)
        xl, hl, lgl = _merge(xl, a_l.reshape(-1, 512), b_l.reshape(-1, 512), c_l.reshape(-1, 512), gl, mods,
                             norm_ffn3, w_proj_a, w_proj_b, w_proj_c, w_out, w_router_t, l,
                             tm=512, row_of_tile=lat_rows[512])
        xg_c, gs_c, pos_c = _route(lgc, hc, nseg=TOK_TILE // n_c)
        xg_l, gs_l, pos_l = _route(lgl, hl, nseg=1)
        og_c, og_l = _expert_ffn(xg_c, xg_l, gs_c, gs_l, w_exp_gate, w_exp_up, w_exp_down, l)
        xc = _scatter(xc, og_c, pos_c.T, mods, norm_final2, l, row_of_tile=ctx_row, final=last)
        xl = _scatter(xl, og_l, pos_l.T, mods, norm_final2, l, row_of_tile=lat_rows[SC_TM], final=last)

    new_ak, new_av, new_bk, new_bv = caches
    return (xc.reshape(nb_c, n_c, D), xl.reshape(nb_l, n_l, D),
            new_ak.reshape(nb_c, DEPTH, n_c, 4, 2, A_DH), new_av.reshape(nb_c, DEPTH, n_c, 4, 128),
            new_bk.reshape(nb_c, DEPTH, n_c, 8, 64), new_bv.reshape(nb_c, DEPTH, n_c, 8, 64),
            states[0], states[1])
```

```python
import functools
import math

import numpy as np
import jax
import jax.numpy as jnp
from jax import lax
from jax.experimental import pallas as pl
from jax.experimental.pallas import tpu as pltpu

F32 = jnp.float32
BF16 = jnp.bfloat16
I32 = jnp.int32

D = 1024
DEPTH = 4
GRID_W = 64
A_DH = 64
N_EXPERTS = 16
EXPERT_FF = 1024
RET_CHUNK = 128
ROPE_BASE = 10000.0
EPS = 1e-6
NEG_INF = -1e30

Z_W = 4608
G_W = 3072
PROJ_TN = 512
N_ZT = Z_W // PROJ_TN
N_GT = G_W // PROJ_TN
COL_AQ, COL_AK, COL_AV = 0, 4, 8
COL_BQ, COL_BK, COL_BV = 12, 16, 20
COL_CQ, COL_CK, COL_CV, COL_CG = 24, 26, 28, 32

NA_QR = 4
NA_KR = 12
NA_HALF = 4
NA_ROWS = 8
NA_COLS = 16

TOK_TILE = 2048
SLOTS = 256
VMEM_LIMIT = 56 * 1024 * 1024


def _params(sem):
    return pltpu.CompilerParams(dimension_semantics=sem, vmem_limit_bytes=VMEM_LIMIT)


def _dot(a, b):
    return jnp.dot(a, b, preferred_element_type=F32)


def _dot_nt(a, b):
    return lax.dot_general(a, b, (((1,), (1,)), ((), ())), preferred_element_type=F32)


def _half_masks(dtype, scale=1.0):
    lane = lax.broadcasted_iota(I32, (1, 128), 1)
    lo = jnp.where(lane < 64, scale, 0.0).astype(dtype)
    hi = jnp.where(lane >= 64, scale, 0.0).astype(dtype)
    return lo, hi


def _mod_kernel(c_ref, w_ref, b_ref, o_ref):
    c = c_ref[...]
    s = (c * jax.nn.sigmoid(c)).astype(BF16)
    o_ref[0] = _dot(s, w_ref[0].astype(BF16)) + b_ref[0]


def _modulation(cvec, w_mod, b_mod):
    tn = 1536
    return pl.pallas_call(
        _mod_kernel,
        grid=(DEPTH, 6 * D // tn),
        in_specs=[pl.BlockSpec((16, D), lambda l, j: (0, 0)),
                  pl.BlockSpec((1, D, tn), lambda l, j: (l, 0, j)),
                  pl.BlockSpec((1, 1, tn), lambda l, j: (l, 0, j))],
        out_specs=pl.BlockSpec((1, 16, tn), lambda l, j: (l, 0, j)),
        out_shape=jax.ShapeDtypeStruct((DEPTH, 16, 6 * D), F32),
        compiler_params=_params(("arbitrary", "arbitrary")),
        name="modulation",
    )(cvec, w_mod, b_mod.reshape(DEPTH, 1, 6 * D))


def _mod_spec(l, which, row_of_tile, ngrid):
    if ngrid == 1:
        return pl.BlockSpec((1, 1, D), lambda i: (l * 16 + row_of_tile(i), 0, which))
    return pl.BlockSpec((1, 1, D), lambda i, j: (l * 16 + row_of_tile(i), 0, which))


def _rope_tables(n):
    half = A_DH // 2
    quarter = half // 2
    freqs = ROPE_BASE ** (-np.arange(quarter, dtype=np.float64) * 2.0 / half)
    t = np.arange(n)
    row = (t // GRID_W).astype(np.float64)[:, None] * freqs
    col = (t % GRID_W).astype(np.float64)[:, None] * freqs
    ang = np.concatenate([row, row, col, col], axis=-1)
    ang = np.concatenate([ang, ang], axis=-1)
    first = (np.arange(128) % 32) < 16
    cos = np.cos(ang)
    sin = np.where(first[None, :], -np.sin(ang), np.sin(ang))
    return jnp.asarray(cos, F32), jnp.asarray(sin, F32)


def _proj_kernel(*refs, tm, rope, emit_cache):
    x_ref, sh_ref, sc_ref, g_ref, w_ref = refs[:5]
    k = 5
    if rope:
        cos_ref, sin_ref = refs[k:k + 2]
        k += 2
    z_ref, gt_ref = refs[k:k + 2]
    k += 2
    if emit_cache:
        cache_refs = refs[k:k + 4]
        k += 4
    h_scr = refs[k]
    j = pl.program_id(1)

    @pl.when(j == 0)
    def _():
        x = x_ref[...]
        y = x * lax.rsqrt(jnp.mean(x * x, axis=-1, keepdims=True) + EPS) * g_ref[0]
        h_scr[...] = (y * (1.0 + sc_ref[0]) + sh_ref[0]).astype(BF16)

    z = _dot(h_scr[...], w_ref[0].astype(BF16))

    if rope:
        @pl.when(j < 2)
        def _():
            lane = lax.broadcasted_iota(I32, (tm, 128), 1)
            first = (lane & 31) < 16
            for cc in range(PROJ_TN // 128):
                zc = z[:, cc * 128:(cc + 1) * 128]
                rot = jnp.where(first, pltpu.roll(zc, 112, 1), pltpu.roll(zc, 16, 1))
                z_ref[:, cc * 128:(cc + 1) * 128] = (zc * cos_ref[...] + rot * sin_ref[...]).astype(BF16)

        @pl.when((j >= 2) & (j < N_ZT))
        def _():
            z_ref[...] = z.astype(BF16)
    else:
        @pl.when(j < N_ZT)
        def _():
            z_ref[...] = z.astype(BF16)

    @pl.when(j >= N_ZT)
    def _():
        gt_ref[...] = z.astype(BF16)

    if emit_cache:
        for tile, ref in zip((1, 2, 4, 5), cache_refs):
            @pl.when(j == tile)
            def _(ref=ref):
                ref[...] = z.reshape(tm // 256, 1, 256, PROJ_TN)


def _project(x, mods, norm_g, w_in, l, *, tm, row_of_tile, rope_tabs=None, caches=None, n_ctx=None):
    t = x.shape[0]
    nt = t // tm
    rope = rope_tabs is not None
    emit_cache = n_ctx is not None
    in_specs = [pl.BlockSpec((tm, D), lambda i, j: (i, 0)),
                _mod_spec(l, 0, row_of_tile, 2), _mod_spec(l, 1, row_of_tile, 2),
                pl.BlockSpec((1, 1, D), lambda i, j: (l, 0, 0)),
                pl.BlockSpec((1, D, PROJ_TN), lambda i, j: (l, 0, j))]
    args = [x, mods, mods, norm_g, w_in]
    if rope:
        in_specs += [pl.BlockSpec((tm, 128), lambda i, j: (0, 0))] * 2
        args += list(rope_tabs)
    out_specs = [pl.BlockSpec((tm, PROJ_TN), lambda i, j: (i, jnp.minimum(j, N_ZT - 1))),
                 pl.BlockSpec((tm, PROJ_TN), lambda i, j: (i, jnp.maximum(j - N_ZT, 0)))]
    out_shape = [jax.ShapeDtypeStruct((t, Z_W), BF16), jax.ShapeDtypeStruct((t, G_W), BF16)]
    aliases = {}
    if emit_cache:
        nb = tm // 256
        out_specs += [pl.BlockSpec((nb, 1, 256, PROJ_TN), lambda i, j: (i, l, 0, 0))] * 4
        out_shape += [jax.ShapeDtypeStruct((n_ctx, DEPTH, 256, PROJ_TN), F32)] * 4
        if caches is not None:
            base = len(args)
            in_specs += [pl.BlockSpec(memory_space=pl.ANY)] * 4
            args += list(caches)
            aliases = {base + a: 2 + a for a in range(4)}
    kern = functools.partial(_proj_kernel, tm=tm, rope=rope, emit_cache=emit_cache)
    if emit_cache and caches is not None:
        def kern(*refs):
            n_in = len(args)
            return _proj_kernel(*refs[:n_in - 4], *refs[n_in:], tm=tm, rope=rope, emit_cache=True)
    return pl.pallas_call(
        kern,
        grid=(nt, N_ZT + N_GT),
        in_specs=in_specs, out_specs=out_specs, out_shape=out_shape,
        scratch_shapes=[pltpu.VMEM((tm, D), BF16)],
        input_output_aliases=aliases,
        compiler_params=_params(("arbitrary", "arbitrary")),
        name="in_proj_ctx" if emit_cache else "in_proj_lat",
    )(*args)


ONES_ROWS = 16

def _diff_lambda(dl_ref, lam_init):
    dl = dl_ref[0]
    la = jnp.sum(dl[0:1] * dl[1:2], axis=-1, keepdims=True)
    lb = jnp.sum(dl[2:3] * dl[3:4], axis=-1, keepdims=True)
    return jnp.exp(la) - jnp.exp(lb) + lam_init


def _diff_lat_kernel(q_ref, k_ref, v_ref, ck_ref, cv_ref, dl_ref, gain_ref, o_ref, vt_s, sa0, sa1, sb0, sb1,
                     *, n, nq, lam_init):
    qi = pl.program_id(2)

    @pl.when(qi == 0)
    def _():
        vt_s[:128, :n] = v_ref[0].astype(F32).T.astype(BF16)
        vt_s[:128, n:] = cv_ref[0, 0].T.astype(BF16)
        vt_s[128:, :] = jnp.ones((ONES_ROWS, vt_s.shape[1]), BF16)

    def scores(dst):
        lo, hi = _half_masks(BF16, A_DH ** -0.5)
        q = q_ref[0]
        kk = k_ref[0]
        ck = ck_ref[0, 0].astype(BF16)
        for msk, ref in zip((lo, hi), dst):
            qh = q * msk
            ref[:n, :] = _dot_nt(kk, qh)
            ref[n:, :] = _dot_nt(ck, qh)

    def finish(src):
        lam = _diff_lambda(dl_ref, lam_init)
        outs = []
        for ref in src:
            s = ref[...]
            e = jnp.exp(s - jnp.max(s, axis=0, keepdims=True)).astype(BF16)
            acc = _dot(vt_s[...], e)
            outs.append(acc[:128] / acc[128:129])
        ot = outs[0] - lam * outs[1]
        on = ot * lax.rsqrt(jnp.mean(ot * ot, axis=0, keepdims=True) + EPS)
        o_ref[0] = ((on * gain_ref[0]) * (1.0 - lam_init)).T.astype(BF16)

    slot_a, slot_b = (sa0, sa1), (sb0, sb1)
    even = qi % 2 == 0

    @pl.when(qi == 0)
    def _():
        scores(slot_a)

    @pl.when((qi > 0) & (qi < nq) & even)
    def _():
        scores(slot_a)
        finish(slot_b)

    @pl.when((qi < nq) & jnp.logical_not(even))
    def _():
        scores(slot_b)
        finish(slot_a)

    @pl.when(qi == nq)
    def _():
        finish(slot_b if nq % 2 == 0 else slot_a)


def _diff_attention_lat(z3, cache_k, cache_v, diff_lambda, diff_norm_col, l, *, tq):
    b, n, _ = z3.shape
    past = cache_k.shape[2]
    nq = n // tq
    lam_init = 0.8 - 0.6 * math.exp(-0.3 * l)
    return pl.pallas_call(
        functools.partial(_diff_lat_kernel, n=n, nq=nq, lam_init=lam_init),
        grid=(b, 4, nq + 1),
        in_specs=[pl.BlockSpec((1, tq, 128), lambda bi, h, qi: (bi, jnp.minimum(qi, nq - 1), COL_AQ + h)),
                  pl.BlockSpec((1, n, 128), lambda bi, h, qi: (bi, 0, COL_AK + h)),
                  pl.BlockSpec((1, n, 128), lambda bi, h, qi: (bi, 0, COL_AV + h)),
                  pl.BlockSpec((1, 1, past, 128), lambda bi, h, qi: (bi, l, 0, h)),
                  pl.BlockSpec((1, 1, past, 128), lambda bi, h, qi: (bi, l, 0, h)),
                  pl.BlockSpec((1, 4, A_DH), lambda bi, h, qi: (l, 0, 0)),
                  pl.BlockSpec((1, 128, 1), lambda bi, h, qi: (l, h, 0))],
        out_specs=pl.BlockSpec((1, tq, 128), lambda bi, h, qi: (bi, jnp.maximum(qi - 1, 0), h)),
        out_shape=jax.ShapeDtypeStruct((b, n, 512), BF16),
        scratch_shapes=[pltpu.VMEM((128 + ONES_ROWS, n + past), BF16)] + [pltpu.VMEM((n + past, tq), F32)] * 4,
        compiler_params=_params(("arbitrary",) * 3),
        name="diff_attn_lat",
    )(z3, z3, z3, cache_k, cache_v, diff_lambda, diff_norm_col)


def _ctx_attn_kernel(aq_ref, ak_ref, av_ref, bq_ref, bk_ref, bv_ref, dl_ref, gain_ref, ao_ref, bo_ref,
                     *, lam_init):
    lo, hi = _half_masks(BF16, A_DH ** -0.5)
    one_lo, one_hi = _half_masks(BF16)
    lam = _diff_lambda(dl_ref, lam_init)
    for h in range(4):
        hs = slice(h * 128, h * 128 + 128)
        q = aq_ref[0, :, hs]
        kk = ak_ref[0, :, hs]
        v = av_ref[0, :, hs]
        outs = []
        for msk in (lo, hi):
            s = _dot_nt(q * msk, kk)
            e = jnp.exp(s - jnp.max(s, axis=-1, keepdims=True))
            outs.append(_dot(e.astype(BF16), v) / jnp.sum(e, axis=-1, keepdims=True))
        o = outs[0] - lam * outs[1]
        on = o * lax.rsqrt(jnp.mean(o * o, axis=-1, keepdims=True) + EPS)
        ao_ref[0, :, hs] = ((on * gain_ref[0, :, hs]) * (1.0 - lam_init)).astype(BF16)

        q = bq_ref[0, :, hs]
        kk = bk_ref[0, :, hs]
        v = bv_ref[0, :, hs]
        acc = None
        for msk, vmsk in ((lo, one_lo), (hi, one_hi)):
            s = _dot_nt(q * msk, kk)
            e = jnp.exp(s - jnp.max(s, axis=-1, keepdims=True))
            o = _dot(e.astype(BF16), v * vmsk) / jnp.sum(e, axis=-1, keepdims=True)
            acc = o if acc is None else acc + o
        bo_ref[0, :, hs] = acc.astype(BF16)


def _ctx_attention(z3, diff_lambda, diff_norm, l):
    b, n, _ = z3.shape
    lam_init = 0.8 - 0.6 * math.exp(-0.3 * l)
    col = lambda c: pl.BlockSpec((1, n, 512), lambda bi: (bi, 0, c))
    return pl.pallas_call(
        functools.partial(_ctx_attn_kernel, lam_init=lam_init),
        grid=(b,),
        in_specs=[col(0), col(1), col(2), col(3), col(4), col(5),
                  pl.BlockSpec((1, 4, A_DH), lambda bi: (l, 0, 0)),
                  pl.BlockSpec((1, 1, 512), lambda bi: (l, 0, 0))],
        out_specs=[pl.BlockSpec((1, n, 512), lambda bi: (bi, 0, 0))] * 2,
        out_shape=[jax.ShapeDtypeStruct((b, n, 512), BF16)] * 2,
        compiler_params=_params(("arbitrary",)),
        name="attn_ctx",
    )(z3, z3, z3, z3, z3, z3, diff_lambda, diff_norm)


def _na_table_kernel(rpb_ref, o_ref):
    l = pl.program_id(0)
    h = pl.program_id(1)
    nrel_r = 2 * NA_ROWS - 1
    nrel_c = 2 * NA_COLS - 1
    base = (l * 8 + h) * (nrel_r * nrel_c)
    kc = lax.broadcasted_iota(I32, (GRID_W, GRID_W), 0)
    qc = lax.broadcasted_iota(I32, (GRID_W, GRID_W), 1)
    dcm = jnp.clip(kc - qc + (NA_COLS - 1), 0, nrel_c - 1)
    ws = jnp.clip(qc - NA_COLS // 2, 0, GRID_W - NA_COLS)
    col_ok = (kc >= ws) & (kc < ws + NA_COLS)
    neg = jnp.full((GRID_W, GRID_W), NEG_INF, F32)
    tabs = []
    for dr in range(nrel_r):
        acc = jnp.zeros((GRID_W, GRID_W), F32)
        for dc in range(nrel_c):
            acc = jnp.where(dcm == dc, rpb_ref[base + dr * nrel_c + dc], acc)
        tabs.append(jnp.where(col_ok, acc, neg))
    for typ in range(3):
        for i in range(NA_QR):
            for j in range(NA_KR):
                if typ == 0:
                    valid, dr = j < NA_ROWS, j - i + 7
                elif typ == 1:
                    valid, dr = i <= j < i + NA_ROWS, j - i + 3
                else:
                    valid, dr = j >= 4, j - i - 1
                o_ref[0, 0, typ, j * 64:(j + 1) * 64, i * 64:(i + 1) * 64] = tabs[dr] if valid else neg


def _na_tables(na_rpb):
    return pl.pallas_call(
        _na_table_kernel,
        grid=(DEPTH, 8),
        in_specs=[pl.BlockSpec(memory_space=pltpu.SMEM)],
        out_specs=pl.BlockSpec((1, 1, 3, NA_KR * 64, NA_QR * 64), lambda l, h: (l, h, 0, 0, 0)),
        out_shape=jax.ShapeDtypeStruct((DEPTH, 8, 3, NA_KR * 64, NA_QR * 64), F32),
        compiler_params=_params(("arbitrary",) * 2),
        name="na_bias_tables",
    )(na_rpb.reshape(-1))


def _na_kernel(q_ref, k_ref, v_ref, ck_ref, cv_ref, tab_ref, o_ref, vt_s, cvt_s, slot_a, slot_b, *, n, nrb):
    rb = pl.program_id(2)
    nwin = NA_KR * GRID_W

    @pl.when(rb == 0)
    def _():
        for pr in range(2):
            ps = slice(pr * 128, pr * 128 + 128)
            vt = v_ref[0, :, ps].astype(F32).T
            for c in range(n // 128):
                vt_s[pr, c] = vt[:, c * 128:(c + 1) * 128].astype(BF16)
            cvt_s[pr] = cv_ref[0, 0, :, ps].T.astype(BF16)

    def first_key_row(block):
        return jnp.clip(block * NA_QR - NA_HALF, 0, 32 - NA_KR)

    def scores(dst):
        typ = jnp.where(rb == 0, 0, jnp.where(rb == nrb - 1, 2, 1))
        off = pl.multiple_of(first_key_row(rb) * GRID_W, 256)
        lo, hi = _half_masks(BF16, 0.125)
        for pr in range(2):
            ps = slice(pr * 128, pr * 128 + 128)
            q = q_ref[0, :, ps]
            kw = k_ref[0, pl.ds(off, nwin), ps]
            ck = ck_ref[0, 0, :, ps].astype(BF16)
            for half, msk in enumerate((lo, hi)):
                qh = q * msk
                dst[pr * 2 + half, :nwin, :] = _dot_nt(kw, qh) + tab_ref[0, pr * 2 + half, typ]
                dst[pr * 2 + half, nwin:, :] = _dot_nt(ck, qh)

    def finish(src):
        c0 = first_key_row(rb - 1) // 2
        for pr in range(2):
            vtw = jnp.concatenate([vt_s[pr, c0 + c] for c in range(nwin // 128)] + [cvt_s[pr]], axis=1)
            outs = []
            for half in range(2):
                s = src[pr * 2 + half]
                e = jnp.exp(s - jnp.max(s, axis=0, keepdims=True)).astype(BF16)
                lhs = jnp.concatenate([vtw[half * 64:half * 64 + 64],
                                       jnp.ones((ONES_ROWS, vtw.shape[1]), BF16)], axis=0)
                acc = _dot(lhs, e)
                outs.append(acc[:64] / acc[64:65])
            o_ref[0, :, pr * 128:pr * 128 + 128] = jnp.concatenate(outs, axis=0).T.astype(BF16)

    even = rb % 2 == 0

    @pl.when(rb == 0)
    def _():
        scores(slot_a)

    @pl.when((rb > 0) & (rb < nrb) & even)
    def _():
        scores(slot_a)
        finish(slot_b)

    @pl.when((rb < nrb) & jnp.logical_not(even))
    def _():
        scores(slot_b)
        finish(slot_a)

    @pl.when(rb == nrb)
    def _():
        finish(slot_b if nrb % 2 == 0 else slot_a)


def _na_attention(z3, cache_k, cache_v, tables, l):
    b, n, _ = z3.shape
    past = cache_k.shape[2]
    tq = NA_QR * GRID_W
    nrb = n // tq
    slot = pltpu.VMEM((4, NA_KR * GRID_W + past, tq), F32)
    return pl.pallas_call(
        functools.partial(_na_kernel, n=n, nrb=nrb),
        grid=(2, b, nrb + 1),
        in_specs=[pl.BlockSpec((1, tq, 256), lambda pp, bi, rb: (bi, jnp.minimum(rb, nrb - 1), COL_BQ // 2 + pp)),
                  pl.BlockSpec((1, n, 256), lambda pp, bi, rb: (bi, 0, COL_BK // 2 + pp)),
                  pl.BlockSpec((1, n, 256), lambda pp, bi, rb: (bi, 0, COL_BV // 2 + pp)),
                  pl.BlockSpec((1, 1, past, 256), lambda pp, bi, rb: (bi, l, 0, pp)),
                  pl.BlockSpec((1, 1, past, 256), lambda pp, bi, rb: (bi, l, 0, pp)),
                  pl.BlockSpec((1, 4, 3, NA_KR * 64, tq), lambda pp, bi, rb: (l, pp, 0, 0, 0))],
        out_specs=pl.BlockSpec((1, tq, 256), lambda pp, bi, rb: (bi, jnp.maximum(rb - 1, 0), pp)),
        out_shape=jax.ShapeDtypeStruct((b, n, 512), BF16),
        scratch_shapes=[pltpu.VMEM((2, n // 128, 128, 128), BF16), pltpu.VMEM((2, 128, past), BF16),
                        slot, slot],
        compiler_params=_params(("arbitrary",) * 3),
        name="na_attn_lat",
    )(z3, z3, z3, cache_k, cache_v, tables)


RET_UNROLL = 8


def _ret_kernel(*refs, **kw):
    for h in range(4):
        _ret_head(*refs, h=h, **kw)


def _ret_head(*refs, h, n, l, has_state, emit_state):
    rd_ref, q_ref, k_ref, v_ref, g_ref, gain_ref = refs[:6]
    i = 6
    if has_state:
        s0f_ref, s0b_ref = refs[i:i + 2]
        i += 2
    o_ref = refs[i]
    i += 1
    if emit_state:
        sf_ref, sb_ref = refs[i:i + 2]
        i += 2
    uf, ub = refs[i:i + 2]
    t = RET_CHUNK
    nc = n // t
    unroll = min(nc, RET_UNROLL)
    half = h % 2
    pair = slice((h // 2) * 128, (h // 2) * 128 + 128)
    head = slice(h * 128, h * 128 + 128)

    def log_gamma(direction):
        x = jnp.full((1, 1), rd_ref[l * 8 + direction * 4 + h], F32)
        return jnp.minimum(x, 0.0) - jnp.log1p(jnp.exp(-jnp.abs(x)))

    lgf = log_gamma(0)
    lgb = log_gamma(1)
    lane = lax.broadcasted_iota(I32, (1, 128), 1)
    hmask = jnp.where((lane >= 64) if half else (lane < 64), 1.0, 0.0)
    ri = lax.broadcasted_iota(I32, (t, t), 0).astype(F32)
    ci = lax.broadcasted_iota(I32, (t, t), 1).astype(F32)
    diff = ri - ci
    dmat = jnp.where(diff > 0, jnp.exp(lgf * jnp.maximum(diff, 0.0)),
                     jnp.where(diff < 0, jnp.exp(lgb * jnp.maximum(-diff, 0.0)), 2.0))
    pos_r = lax.broadcasted_iota(I32, (1, t), 1).astype(F32)
    pos_c = lax.broadcasted_iota(I32, (t, 1), 0).astype(F32)
    upd_f = jnp.exp(lgf * (t - 1.0 - pos_r))
    upd_b = jnp.exp(lgb * pos_r)
    cross_f = jnp.exp(lgf * (pos_c + 1.0))
    cross_b = jnp.exp(lgb * (t - pos_c))
    gch_f = jnp.exp(lgf * t)
    gch_b = jnp.exp(lgb * t)
    kscale = hmask * (64 ** -0.5)

    def chunk(ref, c, lanes):
        return ref[0, pl.ds(pl.multiple_of(c * t, t), t), lanes]

    def summarise(c, carry):
        kt = (chunk(k_ref, c, pair).astype(F32) * kscale).T
        vc = chunk(v_ref, c, head)
        uf[c] = _dot((kt * upd_f).astype(BF16), vc)
        ub[c] = _dot((kt * upd_b).astype(BF16), vc)
        return carry
    lax.fori_loop(0, nc, summarise, 0, unroll=unroll)

    if has_state:
        s0f = jnp.concatenate([s0f_ref[0, 0, h], s0f_ref[0, 0, h]], axis=0)
        s0b = jnp.concatenate([s0b_ref[0, 0, h], s0b_ref[0, 0, h]], axis=0)
    else:
        s0f = jnp.zeros((128, 128), F32)
        s0b = s0f

    def scan_f(c, s):
        u = uf[c]
        uf[c] = s
        return gch_f * s + u
    s_fin_f = lax.fori_loop(0, nc, scan_f, s0f, unroll=True)

    def scan_b(cc, s):
        c = nc - 1 - cc
        u = ub[c]
        ub[c] = s
        return gch_b * s + u
    s_fin_b = lax.fori_loop(0, nc, scan_b, s0b, unroll=True)

    if emit_state:
        sf_ref[0, 0, h] = s_fin_f[half * 64:half * 64 + 64]
        sb_ref[0, 0, h] = s_fin_b[half * 64:half * 64 + 64]

    gain = gain_ref[0, :, head]

    def emit(c, carry):
        qc = (chunk(q_ref, c, pair).astype(F32) * hmask).astype(BF16)
        kc = (chunk(k_ref, c, pair).astype(F32) * kscale).astype(BF16)
        vc = chunk(v_ref, c, head)
        att = (_dot_nt(qc, kc) * dmat).astype(BF16)
        o = (_dot(att, vc)
             + _dot(qc, uf[c].astype(BF16)) * cross_f
             + _dot(qc, ub[c].astype(BF16)) * cross_b)
        mu = jnp.mean(o, axis=-1, keepdims=True)
        var = jnp.mean(jnp.square(o - mu), axis=-1, keepdims=True)
        on = (o - mu) * lax.rsqrt(var + EPS) * gain
        g = chunk(g_ref, c, head).astype(F32)
        o_ref[0, pl.ds(pl.multiple_of(c * t, t), t), head] = (g * jax.nn.sigmoid(g) * on).astype(BF16)
        return carry
    lax.fori_loop(0, nc, emit, 0, unroll=unroll)


def _retention(z3, ret_decay, ret_norm, l, *, s0f=None, s0b=None, states=None, emit_state=False):
    b, n, _ = z3.shape
    has_state = s0f is not None
    in_specs = [pl.BlockSpec(memory_space=pltpu.SMEM),
                pl.BlockSpec((1, n, 256), lambda bi: (bi, 0, COL_CQ // 2)),
                pl.BlockSpec((1, n, 256), lambda bi: (bi, 0, COL_CK // 2)),
                pl.BlockSpec((1, n, 512), lambda bi: (bi, 0, COL_CV // 4)),
                pl.BlockSpec((1, n, 512), lambda bi: (bi, 0, COL_CG // 4)),
                pl.BlockSpec((1, 1, 512), lambda bi: (l, 0, 0))]
    args = [ret_decay.reshape(-1), z3, z3, z3, z3, ret_norm]
    if has_state:
        in_specs += [pl.BlockSpec((1, 1, 4, 64, 128), lambda bi: (bi, l, 0, 0, 0))] * 2
        args += [s0f, s0b]
    out_specs = [pl.BlockSpec((1, n, 512), lambda bi: (bi, 0, 0))]
    out_shape = [jax.ShapeDtypeStruct((b, n, 512), BF16)]
    aliases = {}
    n_in = len(args)
    if emit_state:
        out_specs += [pl.BlockSpec((1, 1, 4, 64, 128), lambda bi: (bi, l, 0, 0, 0))] * 2
        out_shape += [jax.ShapeDtypeStruct((b, DEPTH, 4, 64, 128), F32)] * 2
        if states is not None:
            in_specs += [pl.BlockSpec(memory_space=pl.ANY)] * 2
            args += list(states)
            aliases = {n_in: 1, n_in + 1: 2}
    kw = dict(n=n, l=l, has_state=has_state, emit_state=emit_state)

    def kern(*refs):
        return _ret_kernel(*refs[:n_in], *refs[len(args):], **kw)
    return pl.pallas_call(
        kern,
        grid=(b,),
        in_specs=in_specs, out_specs=out_specs, out_shape=out_shape,
        scratch_shapes=[pltpu.VMEM((n // RET_CHUNK, 128, 128), F32)] * 2,
        input_output_aliases=aliases,
        compiler_params=_params(("arbitrary",)),
        name="retention_ctx" if emit_state else "retention_lat",
    )(*args)


def _merge_kernel(x_ref, a_ref, b_ref, c_ref, gt_ref, wa_ref, wb_ref, wc_ref, wo_ref,
                  g1_ref, sh_ref, sc_ref, ng_ref, wr_ref, xo_ref, h_ref, lg_ref,
                  wa_s, wb_s, wc_s, wo_s, wr_s):
    @pl.when(pl.program_id(0) == 0)
    def _():
        wa_s[...] = wa_ref[0].astype(BF16)
        wb_s[...] = wb_ref[0].astype(BF16)
        wc_s[...] = wc_ref[0].astype(BF16)
        wo_s[...] = wo_ref[0].astype(BF16)
        wr_s[...] = wr_ref[0].astype(BF16)

    merged = None
    for br, (o_ref, w_s) in enumerate(((a_ref, wa_s), (b_ref, wb_s), (c_ref, wc_s))):
        gate = jax.nn.sigmoid(gt_ref[:, br * D:(br + 1) * D].astype(F32))
        term = gate * _dot(o_ref[...], w_s[...])
        merged = term if merged is None else merged + term
    x = x_ref[...] + g1_ref[0] * _dot(merged.astype(BF16), wo_s[...])
    xo_ref[...] = x
    y = x * lax.rsqrt(jnp.mean(x * x, axis=-1, keepdims=True) + EPS) * ng_ref[0]
    h = (y * (1.0 + sc_ref[0]) + sh_ref[0]).astype(BF16)
    h_ref[...] = h
    lg_ref[...] = _dot_nt(wr_s[...], h)


def _merge(x, a_o, b_o, c_o, gates, mods, norm_ffn, w_pa, w_pb, w_pc, w_out, w_router_t, l, *, tm, row_of_tile):
    t = x.shape[0]
    tok = lambda w: pl.BlockSpec((tm, w), lambda i: (i, 0))
    wspec = lambda r, c: pl.BlockSpec((1, r, c), lambda i: (l, 0, 0))
    return pl.pallas_call(
        _merge_kernel,
        grid=(t // tm,),
        in_specs=[tok(D), tok(512), tok(512), tok(512), tok(G_W),
                  wspec(512, D), wspec(512, D), wspec(512, D), wspec(D, D),
                  _mod_spec(l, 2, row_of_tile, 1), _mod_spec(l, 3, row_of_tile, 1),
                  _mod_spec(l, 4, row_of_tile, 1),
                  pl.BlockSpec((1, 1, D), lambda i: (l, 0, 0)), wspec(N_EXPERTS, D)],
        out_specs=[tok(D), tok(D), pl.BlockSpec((N_EXPERTS, tm), lambda i: (0, i))],
        out_shape=[jax.ShapeDtypeStruct((t, D), F32), jax.ShapeDtypeStruct((t, D), BF16),
                   jax.ShapeDtypeStruct((N_EXPERTS, t), F32)],
        scratch_shapes=[pltpu.VMEM((512, D), BF16)] * 3 + [pltpu.VMEM((D, D), BF16),
                                                            pltpu.VMEM((N_EXPERTS, D), BF16)],
        compiler_params=_params(("arbitrary",)),
        name="merge_out_proj",
    )(x, a_o, b_o, c_o, gates, w_pa, w_pb, w_pc, w_out, mods, mods, mods, norm_ffn, w_router_t)


def _route_kernel(lg_ref, h_ref, xg_ref, gs_ref, pos_ref, aff_s, *, nseg):
    seg_len = TOK_TILE // nseg
    cap = seg_len // 8
    cps = seg_len // 256
    lg = lg_ref[...]
    e = jnp.exp(lg - jnp.max(lg, axis=0, keepdims=True))
    aff = e / jnp.sum(e, axis=0, keepdims=True)
    aff_s[...] = aff

    def seg_sums(x):
        return [jnp.sum(x[:, s * seg_len:(s + 1) * seg_len], axis=1, keepdims=True) for s in range(nseg)]

    def seg_spread(vals):
        return jnp.concatenate([jnp.broadcast_to(v, (N_EXPERTS, seg_len)) for v in vals], axis=1)

    def search(it, thr):
        cand = thr | jnp.left_shift(jnp.int32(1), 30 - it)
        ge = aff >= lax.bitcast_convert_type(cand, F32)
        cnt = seg_spread(seg_sums(jnp.where(ge, 1.0, 0.0)))
        return jnp.where(cnt >= cap, cand, thr)
    thr_bits = lax.fori_loop(0, 31, search, jnp.zeros((N_EXPERTS, TOK_TILE), I32))
    thr = lax.bitcast_convert_type(thr_bits, F32)
    thr_next = lax.bitcast_convert_type(thr_bits + 1, F32)

    r = lax.broadcasted_iota(I32, (256, 256), 0)
    c = lax.broadcasted_iota(I32, (256, 256), 1)
    tri = jnp.where(r < c, 1.0, 0.0).astype(BF16)

    def excl_cumsum(x):
        outs = []
        run = None
        for ch in range(TOK_TILE // 256):
            xc = x[:, ch * 256:(ch + 1) * 256]
            p = _dot(xc.astype(BF16), tri)
            if cps > 1:
                if ch % cps == 0:
                    run = jnp.zeros((N_EXPERTS, 1), F32)
                p = p + run
                run = run + jnp.sum(xc, axis=1, keepdims=True)
            outs.append(p)
        return jnp.concatenate(outs, axis=1)

    gt = jnp.where(aff >= thr_next, 1.0, 0.0)
    eq = jnp.where(aff >= thr, 1.0, 0.0) - gt
    need = cap - seg_spread(seg_sums(gt))
    sel = gt + eq * jnp.where(excl_cumsum(eq) < need, 1.0, 0.0)
    seg_base = seg_spread([jnp.full((N_EXPERTS, 1), float(s * cap), F32) for s in range(nseg)])
    slot = (excl_cumsum(sel) + seg_base).astype(I32)
    pos_ref[...] = jnp.where(sel > 0.5, slot, -1)

    slot_iota = lax.broadcasted_iota(I32, (SLOTS, TOK_TILE), 0)
    h = h_ref[...]
    for grp in range(N_EXPERTS // 4):
        onehots = []
        for ex in range(grp * 4, grp * 4 + 4):
            hit = slot_iota == pos_ref[ex:ex + 1, :]
            gs_ref[ex] = jnp.sum(jnp.where(hit, aff_s[ex:ex + 1, :], 0.0), axis=1, keepdims=True)
            onehots.append(jnp.where(hit, 1.0, 0.0).astype(BF16))
        xg = _dot(jnp.concatenate(onehots, axis=0), h)
        xg_ref[grp * 4:grp * 4 + 4] = xg.astype(BF16).reshape(4, SLOTS, D)


def _route(logits_t, h, *, nseg):
    t = h.shape[0]
    nt = t // TOK_TILE
    return pl.pallas_call(
        functools.partial(_route_kernel, nseg=nseg),
        grid=(nt,),
        in_specs=[pl.BlockSpec((N_EXPERTS, TOK_TILE), lambda i: (0, i)),
                  pl.BlockSpec((TOK_TILE, D), lambda i: (i, 0))],
        out_specs=[pl.BlockSpec((N_EXPERTS, SLOTS, D), lambda i: (0, i, 0)),
                   pl.BlockSpec((N_EXPERTS, SLOTS, 1), lambda i: (0, i, 0)),
                   pl.BlockSpec((N_EXPERTS, TOK_TILE), lambda i: (0, i))],
        out_shape=[jax.ShapeDtypeStruct((N_EXPERTS, nt * SLOTS, D), BF16),
                   jax.ShapeDtypeStruct((N_EXPERTS, nt * SLOTS, 1), F32),
                   jax.ShapeDtypeStruct((N_EXPERTS, t), I32)],
        scratch_shapes=[pltpu.VMEM((N_EXPERTS, TOK_TILE), F32)],
        compiler_params=_params(("arbitrary",)),
        name="route_gather",
    )(logits_t, h)


FFN_SPLIT = 2
FFN_CHUNK = 512


def _ffn_kernel(xc_ref, xl_ref, gc_ref, gl_ref, wg_ref, wu_ref, wd_ref, oc_ref, ol_ref, wg_s, wu_s, wd_s):
    @pl.when(pl.program_id(1) == 0)
    def _():
        wg_s[...] = wg_ref[0, 0].astype(BF16)
        wu_s[...] = wu_ref[0, 0].astype(BF16)
        wd_s[...] = wd_ref[0, 0].astype(BF16)

    for x_ref, g_ref, o_ref in ((xc_ref, gc_ref, oc_ref), (xl_ref, gl_ref, ol_ref)):
        rows = x_ref.shape[1]
        step = min(rows, FFN_CHUNK)
        for r0 in range(0, rows, step):
            rs = slice(r0, r0 + step)
            x = x_ref[0, rs, :]
            a = _dot(x, wg_s[...])
            hid = (a * jax.nn.sigmoid(a) * _dot(x, wu_s[...])).astype(BF16)
            o_ref[0, rs, :] = (_dot(hid, wd_s[...]) * g_ref[0, rs, :]).astype(BF16)


def _expert_ffn(xg_c, xg_l, gs_c, gs_l, w_gate, w_up, w_down, l):
    rc = xg_c.shape[1] // FFN_SPLIT
    rl = xg_l.shape[1] // FFN_SPLIT
    assert rc % 16 == 0 and rl % min(rl, FFN_CHUNK) == 0
    row_map = lambda e, m: (e, m, 0)
    wspec = lambda r, c: pl.BlockSpec((1, 1, r, c), lambda e, m: (l, e, 0, 0))
    return pl.pallas_call(
        _ffn_kernel,
        grid=(N_EXPERTS, FFN_SPLIT),
        in_specs=[pl.BlockSpec((1, rc, D), row_map), pl.BlockSpec((1, rl, D), row_map),
                  pl.BlockSpec((1, rc, 1), row_map), pl.BlockSpec((1, rl, 1), row_map),
                  wspec(D, EXPERT_FF), wspec(D, EXPERT_FF), wspec(EXPERT_FF, D)],
        out_specs=[pl.BlockSpec((1, rc, D), row_map), pl.BlockSpec((1, rl, D), row_map)],
        out_shape=[jax.ShapeDtypeStruct(xg_c.shape, BF16), jax.ShapeDtypeStruct(xg_l.shape, BF16)],
        scratch_shapes=[pltpu.VMEM((D, EXPERT_FF), BF16), pltpu.VMEM((D, EXPERT_FF), BF16),
                        pltpu.VMEM((EXPERT_FF, D), BF16)],
        compiler_params=_params(("arbitrary",) * 2),
        name="expert_ffn",
    )(xg_c, xg_l, gs_c, gs_l, w_gate, w_up, w_down)


SC_TM = 1024


def _scatter_kernel(x_ref, og_ref, pt_ref, g2_ref, ng_ref, o_ref, *, final):
    lane = lax.broadcasted_iota(I32, (SC_TM, SLOTS), 1)
    pt = pt_ref[...]
    onehot = jnp.concatenate(
        [jnp.where(lane == pt[:, ex:ex + 1], 1.0, 0.0).astype(BF16) for ex in range(N_EXPERTS)], axis=1)
    upd = _dot(onehot, og_ref[...].reshape(N_EXPERTS * SLOTS, D))
    x = x_ref[...] + g2_ref[0] * upd
    if final:
        x = x * lax.rsqrt(jnp.mean(x * x, axis=-1, keepdims=True) + EPS) * ng_ref[...]
    o_ref[...] = x


def _scatter(x, outg, pos_t, mods, norm_final, l, *, row_of_tile, final):
    t = x.shape[0]
    per = TOK_TILE // SC_TM
    return pl.pallas_call(
        functools.partial(_scatter_kernel, final=final),
        grid=(t // SC_TM,),
        in_specs=[pl.BlockSpec((SC_TM, D), lambda i: (i, 0)),
                  pl.BlockSpec((N_EXPERTS, SLOTS, D), lambda i: (0, i // per, 0)),
                  pl.BlockSpec((SC_TM, N_EXPERTS), lambda i: (i, 0)),
                  _mod_spec(l, 5, row_of_tile, 1),
                  pl.BlockSpec((1, D), lambda i: (0, 0))],
        out_specs=pl.BlockSpec((SC_TM, D), lambda i: (i, 0)),
        out_shape=jax.ShapeDtypeStruct((t, D), F32),
        compiler_params=_params(("arbitrary",)),
        name="scatter_residual",
    )(x, outg, pos_t, mods, norm_final)


def kernel(x_prompt, x_sample, cache_a_k, cache_a_v, cache_b_k, cache_b_v, state_c_fwd, state_c_bwd, c, c_ctx, w_mod, b_mod, norm_mix, norm_ffn, w_in, diff_lambda, diff_norm, na_rpb, ret_decay, ret_norm, w_proj_a, w_proj_b, w_proj_c, w_out, w_router, w_exp_gate, w_exp_up, w_exp_down, norm_final):
    nb_c, n_c, _ = x_prompt.shape
    nb_l, n_l, _ = x_sample.shape
    past = cache_a_k.shape[2]
    assert (n_c, n_l, past) == (256, TOK_TILE, 256) and nb_c % 8 == 0 and nb_l + 1 <= 16

    cvec = jnp.zeros((16, D), F32).at[0].set(c_ctx).at[1:1 + nb_l].set(c)
    mods = _modulation(cvec, w_mod, b_mod).reshape(DEPTH * 16, 1, 6 * D)
    tables = _na_tables(na_rpb)
    rope_tabs = _rope_tables(n_l)

    norm_mix3 = norm_mix.reshape(DEPTH, 1, D)
    norm_ffn3 = norm_ffn.reshape(DEPTH, 1, D)
    diff_norm3 = diff_norm.reshape(DEPTH, 1, 512)
    diff_norm_col = diff_norm.reshape(DEPTH, 512, 1)
    ret_norm3 = ret_norm.reshape(DEPTH, 1, 512)
    w_router_t = jnp.swapaxes(w_router, 1, 2)
    norm_final2 = norm_final.reshape(1, D)
    ca_k = cache_a_k.reshape(nb_l, DEPTH, past, 512)
    ca_v = cache_a_v.reshape(nb_l, DEPTH, past, 512)
    cb_k = cache_b_k.reshape(nb_l, DEPTH, past, 512)
    cb_v = cache_b_v.reshape(nb_l, DEPTH, past, 512)

    ctx_row = lambda i: 0
    lat_rows = {tm: (lambda i, per=n_l // tm: 1 + i // per) for tm in (512, SC_TM, TOK_TILE)}

    xc = x_prompt.reshape(nb_c * n_c, D)
    xl = x_sample.reshape(nb_l * n_l, D)
    caches = None
    states = None
    for l in range(DEPTH):
        last = l == DEPTH - 1
        outs = _project(xc, mods, norm_mix3, w_in, l, tm=1024, row_of_tile=ctx_row, caches=caches, n_ctx=nb_c)
        zc, gc, caches = outs[0], outs[1], outs[2:]
        zl, gl = _project(xl, mods, norm_mix3, w_in, l, tm=TOK_TILE, row_of_tile=lat_rows[TOK_TILE],
                          rope_tabs=rope_tabs)
        zc3 = zc.reshape(nb_c, n_c, Z_W)
        zl3 = zl.reshape(nb_l, n_l, Z_W)
        a_c, b_c = _ctx_attention(zc3, diff_lambda, diff_norm3, l)
        a_l = _diff_attention_lat(zl3, ca_k, ca_v, diff_lambda, diff_norm_col, l, tq=512)
        b_l = _na_attention(zl3, cb_k, cb_v, tables, l)
        rc = _retention(zc3, ret_decay, ret_norm3, l, states=states, emit_state=True)
        c_c, states = rc[0], rc[1:]
        c_l = _retention(zl3, ret_decay, ret_norm3, l, s0f=state_c_fwd, s0b=state_c_bwd)[0]
        xc, hc, lgc = _merge(xc, a_c.reshape(-1, 512), b_c.reshape(-1, 512), c_c.reshape(-1, 512), gc, mods,
                             norm_ffn3, w_proj_a, w_proj_b, w_proj_c, w_out, w_router_t, l,
                             tm=512, row_of_tile=ctx_row)
        xl, hl, lgl = _merge(xl, a_l.reshape(-1, 512), b_l.reshape(-1, 512), c_l.reshape(-1, 512), gl, mods,
                             norm_ffn3, w_proj_a, w_proj_b, w_proj_c, w_out, w_router_t, l,
                             tm=512, row_of_tile=lat_rows[512])
        xg_c, gs_c, pos_c = _route(lgc, hc, nseg=TOK_TILE // n_c)
        xg_l, gs_l, pos_l = _route(lgl, hl, nseg=1)
        og_c, og_l = _expert_ffn(xg_c, xg_l, gs_c, gs_l, w_exp_gate, w_exp_up, w_exp_down, l)
        xc = _scatter(xc, og_c, pos_c.T, mods, norm_final2, l, row_of_tile=ctx_row, final=last)
        xl = _scatter(xl, og_l, pos_l.T, mods, norm_final2, l, row_of_tile=lat_rows[SC_TM], final=last)

    new_ak, new_av, new_bk, new_bv = caches
    return (xc.reshape(nb_c, n_c, D), xl.reshape(nb_l, n_l, D),
            new_ak.reshape(nb_c, DEPTH, n_c, 4, 2, A_DH), new_av.reshape(nb_c, DEPTH, n_c, 4, 128),
            new_bk.reshape(nb_c, DEPTH, n_c, 8, 64), new_bv.reshape(nb_c, DEPTH, n_c, 8, 64),
            states[0], states[1])
```

```python
import functools
import math

import numpy as np
import jax
import jax.numpy as jnp
from jax import lax
from jax.experimental import pallas as pl
from jax.experimental.pallas import tpu as pltpu

F32 = jnp.float32
BF16 = jnp.bfloat16
I32 = jnp.int32

D = 1024
DEPTH = 4
GRID_W = 64
A_DH = 64
N_EXPERTS = 16
EXPERT_FF = 1024
RET_CHUNK = 128
ROPE_BASE = 10000.0
EPS = 1e-6
NEG_INF = -1e30

Z_W = 4608
G_W = 3072
PROJ_TN = 512
PROJ_ROWS = 512
N_ZT = Z_W // PROJ_TN
N_GT = G_W // PROJ_TN
COL_AQ, COL_AK, COL_AV = 0, 4, 8
COL_BQ, COL_BK, COL_BV = 12, 16, 20
COL_CQ, COL_CK, COL_CV, COL_CG = 24, 26, 28, 32

NA_QR = 4
NA_KR = 12
NA_HALF = 4
NA_ROWS = 8
NA_COLS = 16

TOK_TILE = 2048
SLOTS = 256
VMEM_LIMIT = 56 * 1024 * 1024


def _params(sem):
    return pltpu.CompilerParams(dimension_semantics=sem, vmem_limit_bytes=VMEM_LIMIT)


def _dot(a, b):
    return jnp.dot(a, b, preferred_element_type=F32)


def _dot_nt(a, b):
    return lax.dot_general(a, b, (((1,), (1,)), ((), ())), preferred_element_type=F32)


def _half_masks(dtype, scale=1.0):
    lane = lax.broadcasted_iota(I32, (1, 128), 1)
    lo = jnp.where(lane < 64, scale, 0.0).astype(dtype)
    hi = jnp.where(lane >= 64, scale, 0.0).astype(dtype)
    return lo, hi


def _mod_kernel(c_ref, w_ref, b_ref, o_ref):
    c = c_ref[...]
    s = (c * jax.nn.sigmoid(c)).astype(BF16)
    o_ref[0] = _dot(s, w_ref[0].astype(BF16)) + b_ref[0]


def _modulation(cvec, w_mod, b_mod):
    tn = 1536
    return pl.pallas_call(
        _mod_kernel,
        grid=(DEPTH, 6 * D // tn),
        in_specs=[pl.BlockSpec((16, D), lambda l, j: (0, 0)),
                  pl.BlockSpec((1, D, tn), lambda l, j: (l, 0, j)),
                  pl.BlockSpec((1, 1, tn), lambda l, j: (l, 0, j))],
        out_specs=pl.BlockSpec((1, 16, tn), lambda l, j: (l, 0, j)),
        out_shape=jax.ShapeDtypeStruct((DEPTH, 16, 6 * D), F32),
        compiler_params=_params(("arbitrary", "arbitrary")),
        name="modulation",
    )(cvec, w_mod, b_mod.reshape(DEPTH, 1, 6 * D))


def _mod_spec(l, which, row_of_tile, ngrid):
    if ngrid == 1:
        return pl.BlockSpec((1, 1, D), lambda i: (l * 16 + row_of_tile(i), 0, which))
    return pl.BlockSpec((1, 1, D), lambda i, j: (l * 16 + row_of_tile(i), 0, which))


def _rope_tables(n):
    half = A_DH // 2
    quarter = half // 2
    freqs = ROPE_BASE ** (-np.arange(quarter, dtype=np.float64) * 2.0 / half)
    t = np.arange(n)
    row = (t // GRID_W).astype(np.float64)[:, None] * freqs
    col = (t % GRID_W).astype(np.float64)[:, None] * freqs
    ang = np.concatenate([row, row, col, col], axis=-1)
    ang = np.concatenate([ang, ang], axis=-1)
    first = (np.arange(128) % 32) < 16
    cos = np.cos(ang)
    sin = np.where(first[None, :], -np.sin(ang), np.sin(ang))
    return jnp.asarray(cos, F32), jnp.asarray(sin, F32)


def _proj_kernel(*refs, tm, rope, emit_cache):
    x_ref, sh_ref, sc_ref, g_ref, w_ref = refs[:5]
    k = 5
    if rope:
        cos_ref, sin_ref = refs[k:k + 2]
        k += 2
    z_ref, gt_ref = refs[k:k + 2]
    k += 2
    if emit_cache:
        cache_refs = refs[k:k + 4]
        k += 4
    h_scr = refs[k]
    j = pl.program_id(1)

    @pl.when(j == 0)
    def _():
        x = x_ref[...]
        y = x * lax.rsqrt(jnp.mean(x * x, axis=-1, keepdims=True) + EPS) * g_ref[0]
        h_scr[...] = (y * (1.0 + sc_ref[0]) + sh_ref[0]).astype(BF16)

    wb = w_ref[0].astype(BF16)
    rows = min(tm, PROJ_ROWS)

    def project(write):
        for r0 in range(0, tm, rows):
            write(r0, _dot(h_scr[r0:r0 + rows, :], wb))

    def plain(ref):
        def write(r0, z):
            ref[r0:r0 + rows, :] = z.astype(BF16)
        return write

    def rotated(r0, z):
        lane = lax.broadcasted_iota(I32, (rows, 128), 1)
        first = (lane & 31) < 16
        cos = cos_ref[r0:r0 + rows, :]
        sin = sin_ref[r0:r0 + rows, :]
        for cc in range(PROJ_TN // 128):
            zc = z[:, cc * 128:(cc + 1) * 128]
            rot = jnp.where(first, pltpu.roll(zc, 112, 1), pltpu.roll(zc, 16, 1))
            z_ref[r0:r0 + rows, cc * 128:(cc + 1) * 128] = (zc * cos + rot * sin).astype(BF16)

    def cached(ref):
        def write(r0, z):
            z_ref[r0:r0 + rows, :] = z.astype(BF16)
            ref[r0 // 256:(r0 + rows) // 256] = z.reshape(rows // 256, 1, 256, PROJ_TN)
        return write

    special = ()
    if rope:
        special = (0, 1)
        pl.when(j < 2)(lambda: project(rotated))
    if emit_cache:
        special = (1, 2, 4, 5)
        for tile, ref in zip(special, cache_refs):
            pl.when(j == tile)(functools.partial(project, cached(ref)))
    is_special = functools.reduce(jnp.logical_or, [j == t for t in special])
    pl.when((j < N_ZT) & jnp.logical_not(is_special))(lambda: project(plain(z_ref)))
    pl.when(j >= N_ZT)(lambda: project(plain(gt_ref)))


def _project(x, mods, norm_g, w_in, l, *, tm, row_of_tile, rope_tabs=None, caches=None, n_ctx=None):
    t = x.shape[0]
    nt = t // tm
    rope = rope_tabs is not None
    emit_cache = n_ctx is not None
    in_specs = [pl.BlockSpec((tm, D), lambda i, j: (i, 0)),
                _mod_spec(l, 0, row_of_tile, 2), _mod_spec(l, 1, row_of_tile, 2),
                pl.BlockSpec((1, 1, D), lambda i, j: (l, 0, 0)),
                pl.BlockSpec((1, D, PROJ_TN), lambda i, j: (l, 0, j))]
    args = [x, mods, mods, norm_g, w_in]
    if rope:
        in_specs += [pl.BlockSpec((tm, 128), lambda i, j: (0, 0))] * 2
        args += list(rope_tabs)
    out_specs = [pl.BlockSpec((tm, PROJ_TN), lambda i, j: (i, jnp.minimum(j, N_ZT - 1))),
                 pl.BlockSpec((tm, PROJ_TN), lambda i, j: (i, jnp.maximum(j - N_ZT, 0)))]
    out_shape = [jax.ShapeDtypeStruct((t, Z_W), BF16), jax.ShapeDtypeStruct((t, G_W), BF16)]
    aliases = {}
    if emit_cache:
        nb = tm // 256
        out_specs += [pl.BlockSpec((nb, 1, 256, PROJ_TN), lambda i, j: (i, l, 0, 0))] * 4
        out_shape += [jax.ShapeDtypeStruct((n_ctx, DEPTH, 256, PROJ_TN), F32)] * 4
        if caches is not None:
            base = len(args)
            in_specs += [pl.BlockSpec(memory_space=pl.ANY)] * 4
            args += list(caches)
            aliases = {base + a: 2 + a for a in range(4)}
    kern = functools.partial(_proj_kernel, tm=tm, rope=rope, emit_cache=emit_cache)
    if emit_cache and caches is not None:
        def kern(*refs):
            n_in = len(args)
            return _proj_kernel(*refs[:n_in - 4], *refs[n_in:], tm=tm, rope=rope, emit_cache=True)
    return pl.pallas_call(
        kern,
        grid=(nt, N_ZT + N_GT),
        in_specs=in_specs, out_specs=out_specs, out_shape=out_shape,
        scratch_shapes=[pltpu.VMEM((tm, D), BF16)],
        input_output_aliases=aliases,
        compiler_params=_params(("arbitrary", "arbitrary")),
        name="in_proj_ctx" if emit_cache else "in_proj_lat",
    )(*args)


ONES_ROWS = 16

def _diff_lambda(dl_ref, lam_init):
    dl = dl_ref[0]
    la = jnp.sum(dl[0:1] * dl[1:2], axis=-1, keepdims=True)
    lb = jnp.sum(dl[2:3] * dl[3:4], axis=-1, keepdims=True)
    return jnp.exp(la) - jnp.exp(lb) + lam_init


def _diff_lat_kernel(q_ref, k_ref, v_ref, ck_ref, cv_ref, dl_ref, gain_ref, o_ref, vt_s, *, n, lam_init):
    @pl.when(pl.program_id(2) == 0)
    def _():
        vt_s[:, :n] = v_ref[0].astype(F32).T.astype(BF16)
        vt_s[:, n:] = cv_ref[0, 0].T.astype(BF16)

    lo, hi = _half_masks(BF16, A_DH ** -0.5)
    kk = k_ref[0]
    ck = ck_ref[0, 0].astype(BF16)
    lam = _diff_lambda(dl_ref, lam_init)
    q = q_ref[0]
    es, dens = [], []
    for msk in (lo, hi):
        qh = q * msk
        s = _dot_nt(kk, qh)
        s2 = _dot_nt(ck, qh)
        m = jnp.maximum(jnp.max(s, axis=0, keepdims=True), jnp.max(s2, axis=0, keepdims=True))
        e = jnp.exp(s - m)
        e2 = jnp.exp(s2 - m)
        es.append((e, e2))
        dens.append(jnp.sum(e, axis=0, keepdims=True) + jnp.sum(e2, axis=0, keepdims=True))
    r = lam * dens[0] / dens[1]
    a = (es[0][0] - r * es[1][0]).astype(BF16)
    a2 = (es[0][1] - r * es[1][1]).astype(BF16)
    ot = (_dot(vt_s[:, :n], a) + _dot(vt_s[:, n:], a2)) / dens[0]
    on = ot * lax.rsqrt(jnp.mean(ot * ot, axis=0, keepdims=True) + EPS)
    o_ref[0] = ((on * gain_ref[0]) * (1.0 - lam_init)).T.astype(BF16)


def _diff_attention_lat(z3, cache_k, cache_v, diff_lambda, diff_norm_col, l, *, tq):
    b, n, _ = z3.shape
    past = cache_k.shape[2]
    lam_init = 0.8 - 0.6 * math.exp(-0.3 * l)
    return pl.pallas_call(
        functools.partial(_diff_lat_kernel, n=n, lam_init=lam_init),
        grid=(b, 4, n // tq),
        in_specs=[pl.BlockSpec((1, tq, 128), lambda bi, h, qi: (bi, qi, COL_AQ + h)),
                  pl.BlockSpec((1, n, 128), lambda bi, h, qi: (bi, 0, COL_AK + h)),
                  pl.BlockSpec((1, n, 128), lambda bi, h, qi: (bi, 0, COL_AV + h)),
                  pl.BlockSpec((1, 1, past, 128), lambda bi, h, qi: (bi, l, 0, h)),
                  pl.BlockSpec((1, 1, past, 128), lambda bi, h, qi: (bi, l, 0, h)),
                  pl.BlockSpec((1, 4, A_DH), lambda bi, h, qi: (l, 0, 0)),
                  pl.BlockSpec((1, 128, 1), lambda bi, h, qi: (l, h, 0))],
        out_specs=pl.BlockSpec((1, tq, 128), lambda bi, h, qi: (bi, qi, h)),
        out_shape=jax.ShapeDtypeStruct((b, n, 512), BF16),
        scratch_shapes=[pltpu.VMEM((128, n + past), BF16)],
        compiler_params=_params(("arbitrary",) * 3),
        name="diff_attn_lat",
    )(z3, z3, z3, cache_k, cache_v, diff_lambda, diff_norm_col)


def _ctx_attn_kernel(aq_ref, ak_ref, av_ref, bq_ref, bk_ref, bv_ref, dl_ref, gain_ref, ao_ref, bo_ref,
                     *, lam_init):
    lo, hi = _half_masks(BF16, A_DH ** -0.5)
    one_lo, one_hi = _half_masks(BF16)
    lam = _diff_lambda(dl_ref, lam_init)
    for h in range(4):
        hs = slice(h * 128, h * 128 + 128)
        q = aq_ref[0, :, hs]
        kk = ak_ref[0, :, hs]
        v = av_ref[0, :, hs]
        outs = []
        for msk in (lo, hi):
            s = _dot_nt(q * msk, kk)
            e = jnp.exp(s - jnp.max(s, axis=-1, keepdims=True))
            outs.append(_dot(e.astype(BF16), v) / jnp.sum(e, axis=-1, keepdims=True))
        o = outs[0] - lam * outs[1]
        on = o * lax.rsqrt(jnp.mean(o * o, axis=-1, keepdims=True) + EPS)
        ao_ref[0, :, hs] = ((on * gain_ref[0, :, hs]) * (1.0 - lam_init)).astype(BF16)

        q = bq_ref[0, :, hs]
        kk = bk_ref[0, :, hs]
        v = bv_ref[0, :, hs]
        acc = None
        for msk, vmsk in ((lo, one_lo), (hi, one_hi)):
            s = _dot_nt(q * msk, kk)
            e = jnp.exp(s - jnp.max(s, axis=-1, keepdims=True))
            o = _dot(e.astype(BF16), v * vmsk) / jnp.sum(e, axis=-1, keepdims=True)
            acc = o if acc is None else acc + o
        bo_ref[0, :, hs] = acc.astype(BF16)


def _ctx_attention(z3, diff_lambda, diff_norm, l):
    b, n, _ = z3.shape
    lam_init = 0.8 - 0.6 * math.exp(-0.3 * l)
    col = lambda c: pl.BlockSpec((1, n, 512), lambda bi: (bi, 0, c))
    return pl.pallas_call(
        functools.partial(_ctx_attn_kernel, lam_init=lam_init),
        grid=(b,),
        in_specs=[col(0), col(1), col(2), col(3), col(4), col(5),
                  pl.BlockSpec((1, 4, A_DH), lambda bi: (l, 0, 0)),
                  pl.BlockSpec((1, 1, 512), lambda bi: (l, 0, 0))],
        out_specs=[pl.BlockSpec((1, n, 512), lambda bi: (bi, 0, 0))] * 2,
        out_shape=[jax.ShapeDtypeStruct((b, n, 512), BF16)] * 2,
        compiler_params=_params(("arbitrary",)),
        name="attn_ctx",
    )(z3, z3, z3, z3, z3, z3, diff_lambda, diff_norm)


def _na_table_kernel(rpb_ref, o_ref):
    l = pl.program_id(0)
    h = pl.program_id(1)
    nrel_r = 2 * NA_ROWS - 1
    nrel_c = 2 * NA_COLS - 1
    base = (l * 8 + h) * (nrel_r * nrel_c)
    kc = lax.broadcasted_iota(I32, (GRID_W, GRID_W), 0)
    qc = lax.broadcasted_iota(I32, (GRID_W, GRID_W), 1)
    dcm = jnp.clip(kc - qc + (NA_COLS - 1), 0, nrel_c - 1)
    ws = jnp.clip(qc - NA_COLS // 2, 0, GRID_W - NA_COLS)
    col_ok = (kc >= ws) & (kc < ws + NA_COLS)
    neg = jnp.full((GRID_W, GRID_W), NEG_INF, F32)
    tabs = []
    for dr in range(nrel_r):
        acc = jnp.zeros((GRID_W, GRID_W), F32)
        for dc in range(nrel_c):
            acc = jnp.where(dcm == dc, rpb_ref[base + dr * nrel_c + dc], acc)
        tabs.append(jnp.where(col_ok, acc, neg))
    for typ in range(3):
        for i in range(NA_QR):
            for j in range(NA_KR):
                if typ == 0:
                    valid, dr = j < NA_ROWS, j - i + 7
                elif typ == 1:
                    valid, dr = i <= j < i + NA_ROWS, j - i + 3
                else:
                    valid, dr = j >= 4, j - i - 1
                o_ref[0, 0, typ, j * 64:(j + 1) * 64, i * 64:(i + 1) * 64] = tabs[dr] if valid else neg


def _na_tables(na_rpb):
    return pl.pallas_call(
        _na_table_kernel,
        grid=(DEPTH, 8),
        in_specs=[pl.BlockSpec(memory_space=pltpu.SMEM)],
        out_specs=pl.BlockSpec((1, 1, 3, NA_KR * 64, NA_QR * 64), lambda l, h: (l, h, 0, 0, 0)),
        out_shape=jax.ShapeDtypeStruct((DEPTH, 8, 3, NA_KR * 64, NA_QR * 64), F32),
        compiler_params=_params(("arbitrary",) * 2),
        name="na_bias_tables",
    )(na_rpb.reshape(-1))


def _na_kernel(q_ref, k_ref, v_ref, ck_ref, cv_ref, tab_ref, o_ref, vt_s, cvt_s, slot_a, slot_b, *, n, nrb):
    rb = pl.program_id(2)
    nwin = NA_KR * GRID_W

    @pl.when(rb == 0)
    def _():
        for pr in range(2):
            ps = slice(pr * 128, pr * 128 + 128)
            vt = v_ref[0, :, ps].astype(F32).T
            for c in range(n // 128):
                vt_s[pr, c] = vt[:, c * 128:(c + 1) * 128].astype(BF16)
            cvt_s[pr] = cv_ref[0, 0, :, ps].T.astype(BF16)

    def first_key_row(block):
        return jnp.clip(block * NA_QR - NA_HALF, 0, 32 - NA_KR)

    def scores(dst):
        typ = jnp.where(rb == 0, 0, jnp.where(rb == nrb - 1, 2, 1))
        off = pl.multiple_of(first_key_row(rb) * GRID_W, 256)
        lo, hi = _half_masks(BF16, 0.125)
        for pr in range(2):
            ps = slice(pr * 128, pr * 128 + 128)
            q = q_ref[0, :, ps]
            kw = k_ref[0, pl.ds(off, nwin), ps]
            ck = ck_ref[0, 0, :, ps].astype(BF16)
            for half, msk in enumerate((lo, hi)):
                qh = q * msk
                dst[pr * 2 + half, :nwin, :] = _dot_nt(kw, qh) + tab_ref[0, pr * 2 + half, typ]
                dst[pr * 2 + half, nwin:, :] = _dot_nt(ck, qh)

    def finish(src):
        c0 = first_key_row(rb - 1) // 2
        for pr in range(2):
            vtw = jnp.concatenate([vt_s[pr, c0 + c] for c in range(nwin // 128)] + [cvt_s[pr]], axis=1)
            outs = []
            for half in range(2):
                s = src[pr * 2 + half]
                e = jnp.exp(s - jnp.max(s, axis=0, keepdims=True)).astype(BF16)
                lhs = jnp.concatenate([vtw[half * 64:half * 64 + 64],
                                       jnp.ones((ONES_ROWS, vtw.shape[1]), BF16)], axis=0)
                acc = _dot(lhs, e)
                outs.append(acc[:64] / acc[64:65])
            o_ref[0, :, pr * 128:pr * 128 + 128] = jnp.concatenate(outs, axis=0).T.astype(BF16)

    even = rb % 2 == 0

    @pl.when(rb == 0)
    def _():
        scores(slot_a)

    @pl.when((rb > 0) & (rb < nrb) & even)
    def _():
        scores(slot_a)
        finish(slot_b)

    @pl.when((rb < nrb) & jnp.logical_not(even))
    def _():
        scores(slot_b)
        finish(slot_a)

    @pl.when(rb == nrb)
    def _():
        finish(slot_b if nrb % 2 == 0 else slot_a)


def _na_attention(z3, cache_k, cache_v, tables, l):
    b, n, _ = z3.shape
    past = cache_k.shape[2]
    tq = NA_QR * GRID_W
    nrb = n // tq
    slot = pltpu.VMEM((4, NA_KR * GRID_W + past, tq), F32)
    return pl.pallas_call(
        functools.partial(_na_kernel, n=n, nrb=nrb),
        grid=(2, b, nrb + 1),
        in_specs=[pl.BlockSpec((1, tq, 256), lambda pp, bi, rb: (bi, jnp.minimum(rb, nrb - 1), COL_BQ // 2 + pp)),
                  pl.BlockSpec((1, n, 256), lambda pp, bi, rb: (bi, 0, COL_BK // 2 + pp)),
                  pl.BlockSpec((1, n, 256), lambda pp, bi, rb: (bi, 0, COL_BV // 2 + pp)),
                  pl.BlockSpec((1, 1, past, 256), lambda pp, bi, rb: (bi, l, 0, pp)),
                  pl.BlockSpec((1, 1, past, 256), lambda pp, bi, rb: (bi, l, 0, pp)),
                  pl.BlockSpec((1, 4, 3, NA_KR * 64, tq), lambda pp, bi, rb: (l, pp, 0, 0, 0))],
        out_specs=pl.BlockSpec((1, tq, 256), lambda pp, bi, rb: (bi, jnp.maximum(rb - 1, 0), pp)),
        out_shape=jax.ShapeDtypeStruct((b, n, 512), BF16),
        scratch_shapes=[pltpu.VMEM((2, n // 128, 128, 128), BF16), pltpu.VMEM((2, 128, past), BF16),
                        slot, slot],
        compiler_params=_params(("arbitrary",) * 3),
        name="na_attn_lat",
    )(z3, z3, z3, cache_k, cache_v, tables)


RET_UNROLL = 8


def _ret_kernel(*refs, **kw):
    for h in range(4):
        _ret_head(*refs, h=h, **kw)


def _ret_head(*refs, h, n, l, has_state, emit_state):
    rd_ref, q_ref, k_ref, v_ref, g_ref, gain_ref = refs[:6]
    i = 6
    if has_state:
        s0f_ref, s0b_ref = refs[i:i + 2]
        i += 2
    o_ref = refs[i]
    i += 1
    if emit_state:
        sf_ref, sb_ref = refs[i:i + 2]
        i += 2
    uf, ub = refs[i:i + 2]
    t = RET_CHUNK
    nc = n // t
    unroll = min(nc, RET_UNROLL)
    half = h % 2
    pair = slice((h // 2) * 128, (h // 2) * 128 + 128)
    head = slice(h * 128, h * 128 + 128)

    def log_gamma(direction):
        x = jnp.full((1, 1), rd_ref[l * 8 + direction * 4 + h], F32)
        return jnp.minimum(x, 0.0) - jnp.log1p(jnp.exp(-jnp.abs(x)))

    lgf = log_gamma(0)
    lgb = log_gamma(1)
    lane = lax.broadcasted_iota(I32, (1, 128), 1)
    hmask = jnp.where((lane >= 64) if half else (lane < 64), 1.0, 0.0)
    ri = lax.broadcasted_iota(I32, (t, t), 0).astype(F32)
    ci = lax.broadcasted_iota(I32, (t, t), 1).astype(F32)
    diff = ri - ci
    dmat = jnp.where(diff > 0, jnp.exp(lgf * jnp.maximum(diff, 0.0)),
                     jnp.where(diff < 0, jnp.exp(lgb * jnp.maximum(-diff, 0.0)), 2.0))
    pos_r = lax.broadcasted_iota(I32, (1, t), 1).astype(F32)
    pos_c = lax.broadcasted_iota(I32, (t, 1), 0).astype(F32)
    upd_f = jnp.exp(lgf * (t - 1.0 - pos_r))
    upd_b = jnp.exp(lgb * pos_r)
    cross_f = jnp.exp(lgf * (pos_c + 1.0))
    cross_b = jnp.exp(lgb * (t - pos_c))
    gch_f = jnp.exp(lgf * t)
    gch_b = jnp.exp(lgb * t)
    kscale = hmask * (64 ** -0.5)

    def chunk(ref, c, lanes):
        return ref[0, pl.ds(pl.multiple_of(c * t, t), t), lanes]

    def summarise(c, carry):
        kt = (chunk(k_ref, c, pair).astype(F32) * kscale).T
        vc = chunk(v_ref, c, head)
        uf[c] = _dot((kt * upd_f).astype(BF16), vc)
        ub[c] = _dot((kt * upd_b).astype(BF16), vc)
        return carry
    lax.fori_loop(0, nc, summarise, 0, unroll=unroll)

    if has_state:
        s0f = jnp.concatenate([s0f_ref[0, 0, h], s0f_ref[0, 0, h]], axis=0)
        s0b = jnp.concatenate([s0b_ref[0, 0, h], s0b_ref[0, 0, h]], axis=0)
    else:
        s0f = jnp.zeros((128, 128), F32)
        s0b = s0f

    def scan_f(c, s):
        u = uf[c]
        uf[c] = s
        return gch_f * s + u
    s_fin_f = lax.fori_loop(0, nc, scan_f, s0f, unroll=True)

    def scan_b(cc, s):
        c = nc - 1 - cc
        u = ub[c]
        ub[c] = s
        return gch_b * s + u
    s_fin_b = lax.fori_loop(0, nc, scan_b, s0b, unroll=True)

    if emit_state:
        sf_ref[0, 0, h] = s_fin_f[half * 64:half * 64 + 64]
        sb_ref[0, 0, h] = s_fin_b[half * 64:half * 64 + 64]

    gain = gain_ref[0, :, head]

    def emit(c, carry):
        qc = (chunk(q_ref, c, pair).astype(F32) * hmask).astype(BF16)
        kc = (chunk(k_ref, c, pair).astype(F32) * kscale).astype(BF16)
        vc = chunk(v_ref, c, head)
        att = (_dot_nt(qc, kc) * dmat).astype(BF16)
        o = (_dot(att, vc)
             + _dot(qc, uf[c].astype(BF16)) * cross_f
             + _dot(qc, ub[c].astype(BF16)) * cross_b)
        mu = jnp.mean(o, axis=-1, keepdims=True)
        var = jnp.mean(jnp.square(o - mu), axis=-1, keepdims=True)
        on = (o - mu) * lax.rsqrt(var + EPS) * gain
        g = chunk(g_ref, c, head).astype(F32)
        o_ref[0, pl.ds(pl.multiple_of(c * t, t), t), head] = (g * jax.nn.sigmoid(g) * on).astype(BF16)
        return carry
    lax.fori_loop(0, nc, emit, 0, unroll=unroll)


def _retention(z3, ret_decay, ret_norm, l, *, s0f=None, s0b=None, states=None, emit_state=False):
    b, n, _ = z3.shape
    has_state = s0f is not None
    in_specs = [pl.BlockSpec(memory_space=pltpu.SMEM),
                pl.BlockSpec((1, n, 256), lambda bi: (bi, 0, COL_CQ // 2)),
                pl.BlockSpec((1, n, 256), lambda bi: (bi, 0, COL_CK // 2)),
                pl.BlockSpec((1, n, 512), lambda bi: (bi, 0, COL_CV // 4)),
                pl.BlockSpec((1, n, 512), lambda bi: (bi, 0, COL_CG // 4)),
                pl.BlockSpec((1, 1, 512), lambda bi: (l, 0, 0))]
    args = [ret_decay.reshape(-1), z3, z3, z3, z3, ret_norm]
    if has_state:
        in_specs += [pl.BlockSpec((1, 1, 4, 64, 128), lambda bi: (bi, l, 0, 0, 0))] * 2
        args += [s0f, s0b]
    out_specs = [pl.BlockSpec((1, n, 512), lambda bi: (bi, 0, 0))]
    out_shape = [jax.ShapeDtypeStruct((b, n, 512), BF16)]
    aliases = {}
    n_in = len(args)
    if emit_state:
        out_specs += [pl.BlockSpec((1, 1, 4, 64, 128), lambda bi: (bi, l, 0, 0, 0))] * 2
        out_shape += [jax.ShapeDtypeStruct((b, DEPTH, 4, 64, 128), F32)] * 2
        if states is not None:
            in_specs += [pl.BlockSpec(memory_space=pl.ANY)] * 2
            args += list(states)
            aliases = {n_in: 1, n_in + 1: 2}
    kw = dict(n=n, l=l, has_state=has_state, emit_state=emit_state)

    def kern(*refs):
        return _ret_kernel(*refs[:n_in], *refs[len(args):], **kw)
    return pl.pallas_call(
        kern,
        grid=(b,),
        in_specs=in_specs, out_specs=out_specs, out_shape=out_shape,
        scratch_shapes=[pltpu.VMEM((n // RET_CHUNK, 128, 128), F32)] * 2,
        input_output_aliases=aliases,
        compiler_params=_params(("arbitrary",)),
        name="retention_ctx" if emit_state else "retention_lat",
    )(*args)


def _merge_kernel(x_ref, a_ref, b_ref, c_ref, gt_ref, wa_ref, wb_ref, wc_ref, wo_ref,
                  g1_ref, sh_ref, sc_ref, ng_ref, wr_ref, xo_ref, h_ref, lg_ref,
                  wa_s, wb_s, wc_s, wo_s, wr_s):
    @pl.when(pl.program_id(0) == 0)
    def _():
        wa_s[...] = wa_ref[0].astype(BF16)
        wb_s[...] = wb_ref[0].astype(BF16)
        wc_s[...] = wc_ref[0].astype(BF16)
        wo_s[...] = wo_ref[0].astype(BF16)
        wr_s[...] = wr_ref[0].astype(BF16)

    merged = None
    for br, (o_ref, w_s) in enumerate(((a_ref, wa_s), (b_ref, wb_s), (c_ref, wc_s))):
        gate = jax.nn.sigmoid(gt_ref[:, br * D:(br + 1) * D].astype(F32))
        term = gate * _dot(o_ref[...], w_s[...])
        merged = term if merged is None else merged + term
    x = x_ref[...] + g1_ref[0] * _dot(merged.astype(BF16), wo_s[...])
    xo_ref[...] = x
    y = x * lax.rsqrt(jnp.mean(x * x, axis=-1, keepdims=True) + EPS) * ng_ref[0]
    h = (y * (1.0 + sc_ref[0]) + sh_ref[0]).astype(BF16)
    h_ref[...] = h
    lg_ref[...] = _dot_nt(wr_s[...], h)


def _merge(x, a_o, b_o, c_o, gates, mods, norm_ffn, w_pa, w_pb, w_pc, w_out, w_router_t, l, *, tm, row_of_tile):
    t = x.shape[0]
    tok = lambda w: pl.BlockSpec((tm, w), lambda i: (i, 0))
    wspec = lambda r, c: pl.BlockSpec((1, r, c), lambda i: (l, 0, 0))
    return pl.pallas_call(
        _merge_kernel,
        grid=(t // tm,),
        in_specs=[tok(D), tok(512), tok(512), tok(512), tok(G_W),
                  wspec(512, D), wspec(512, D), wspec(512, D), wspec(D, D),
                  _mod_spec(l, 2, row_of_tile, 1), _mod_spec(l, 3, row_of_tile, 1),
                  _mod_spec(l, 4, row_of_tile, 1),
                  pl.BlockSpec((1, 1, D), lambda i: (l, 0, 0)), wspec(N_EXPERTS, D)],
        out_specs=[tok(D), tok(D), pl.BlockSpec((N_EXPERTS, tm), lambda i: (0, i))],
        out_shape=[jax.ShapeDtypeStruct((t, D), F32), jax.ShapeDtypeStruct((t, D), BF16),
                   jax.ShapeDtypeStruct((N_EXPERTS, t), F32)],
        scratch_shapes=[pltpu.VMEM((512, D), BF16)] * 3 + [pltpu.VMEM((D, D), BF16),
                                                            pltpu.VMEM((N_EXPERTS, D), BF16)],
        compiler_params=_params(("arbitrary",)),
        name="merge_out_proj",
    )(x, a_o, b_o, c_o, gates, w_pa, w_pb, w_pc, w_out, mods, mods, mods, norm_ffn, w_router_t)


def _route_kernel(lg_ref, h_ref, xg_ref, gs_ref, pos_ref, aff_s, *, nseg):
    seg_len = TOK_TILE // nseg
    cap = seg_len // 8
    cps = seg_len // 256
    lg = lg_ref[...]
    e = jnp.exp(lg - jnp.max(lg, axis=0, keepdims=True))
    aff = e / jnp.sum(e, axis=0, keepdims=True)
    aff_s[...] = aff

    def seg_sums(x):
        return [jnp.sum(x[:, s * seg_len:(s + 1) * seg_len], axis=1, keepdims=True) for s in range(nseg)]

    def seg_spread(vals):
        return jnp.concatenate([jnp.broadcast_to(v, (N_EXPERTS, seg_len)) for v in vals], axis=1)

    def search(it, thr):
        cand = thr | jnp.left_shift(jnp.int32(1), 30 - it)
        ge = aff >= lax.bitcast_convert_type(cand, F32)
        cnt = seg_spread(seg_sums(jnp.where(ge, 1.0, 0.0)))
        return jnp.where(cnt >= cap, cand, thr)
    thr_bits = lax.fori_loop(0, 31, search, jnp.zeros((N_EXPERTS, TOK_TILE), I32))
    thr = lax.bitcast_convert_type(thr_bits, F32)
    thr_next = lax.bitcast_convert_type(thr_bits + 1, F32)

    r = lax.broadcasted_iota(I32, (256, 256), 0)
    c = lax.broadcasted_iota(I32, (256, 256), 1)
    tri = jnp.where(r < c, 1.0, 0.0).astype(BF16)

    def excl_cumsum(x):
        outs = []
        run = None
        for ch in range(TOK_TILE // 256):
            xc = x[:, ch * 256:(ch + 1) * 256]
            p = _dot(xc.astype(BF16), tri)
            if cps > 1:
                if ch % cps == 0:
                    run = jnp.zeros((N_EXPERTS, 1), F32)
                p = p + run
                run = run + jnp.sum(xc, axis=1, keepdims=True)
            outs.append(p)
        return jnp.concatenate(outs, axis=1)

    gt = jnp.where(aff >= thr_next, 1.0, 0.0)
    eq = jnp.where(aff >= thr, 1.0, 0.0) - gt
    need = cap - seg_spread(seg_sums(gt))
    sel = gt + eq * jnp.where(excl_cumsum(eq) < need, 1.0, 0.0)
    seg_base = seg_spread([jnp.full((N_EXPERTS, 1), float(s * cap), F32) for s in range(nseg)])
    slot = (excl_cumsum(sel) + seg_base).astype(I32)
    pos_ref[...] = jnp.where(sel > 0.5, slot, -1)

    slot_iota = lax.broadcasted_iota(I32, (SLOTS, TOK_TILE), 0)
    h = h_ref[...]
    for grp in range(N_EXPERTS // 4):
        onehots = []
        for ex in range(grp * 4, grp * 4 + 4):
            hit = slot_iota == pos_ref[ex:ex + 1, :]
            gs_ref[ex] = jnp.sum(jnp.where(hit, aff_s[ex:ex + 1, :], 0.0), axis=1, keepdims=True)
            onehots.append(jnp.where(hit, 1.0, 0.0).astype(BF16))
        xg = _dot(jnp.concatenate(onehots, axis=0), h)
        xg_ref[grp * 4:grp * 4 + 4] = xg.astype(BF16).reshape(4, SLOTS, D)


def _route(logits_t, h, *, nseg):
    t = h.shape[0]
    nt = t // TOK_TILE
    return pl.pallas_call(
        functools.partial(_route_kernel, nseg=nseg),
        grid=(nt,),
        in_specs=[pl.BlockSpec((N_EXPERTS, TOK_TILE), lambda i: (0, i)),
                  pl.BlockSpec((TOK_TILE, D), lambda i: (i, 0))],
        out_specs=[pl.BlockSpec((N_EXPERTS, SLOTS, D), lambda i: (0, i, 0)),
                   pl.BlockSpec((N_EXPERTS, SLOTS, 1), lambda i: (0, i, 0)),
                   pl.BlockSpec((N_EXPERTS, TOK_TILE), lambda i: (0, i))],
        out_shape=[jax.ShapeDtypeStruct((N_EXPERTS, nt * SLOTS, D), BF16),
                   jax.ShapeDtypeStruct((N_EXPERTS, nt * SLOTS, 1), F32),
                   jax.ShapeDtypeStruct((N_EXPERTS, t), I32)],
        scratch_shapes=[pltpu.VMEM((N_EXPERTS, TOK_TILE), F32)],
        compiler_params=_params(("arbitrary",)),
        name="route_gather",
    )(logits_t, h)


FFN_SPLIT = 2
FFN_CHUNK = 512


def _ffn_kernel(xc_ref, xl_ref, gc_ref, gl_ref, wg_ref, wu_ref, wd_ref, oc_ref, ol_ref, wg_s, wu_s, wd_s):
    @pl.when(pl.program_id(1) == 0)
    def _():
        wg_s[...] = wg_ref[0, 0].astype(BF16)
        wu_s[...] = wu_ref[0, 0].astype(BF16)
        wd_s[...] = wd_ref[0, 0].astype(BF16)

    for x_ref, g_ref, o_ref in ((xc_ref, gc_ref, oc_ref), (xl_ref, gl_ref, ol_ref)):
        rows = x_ref.shape[1]
        step = min(rows, FFN_CHUNK)
        for r0 in range(0, rows, step):
            rs = slice(r0, r0 + step)
            x = x_ref[0, rs, :]
            a = _dot(x, wg_s[...])
            hid = (a * jax.nn.sigmoid(a) * _dot(x, wu_s[...])).astype(BF16)
            o_ref[0, rs, :] = (_dot(hid, wd_s[...]) * g_ref[0, rs, :]).astype(BF16)


def _expert_ffn(xg_c, xg_l, gs_c, gs_l, w_gate, w_up, w_down, l):
    rc = xg_c.shape[1] // FFN_SPLIT
    rl = xg_l.shape[1] // FFN_SPLIT
    assert rc % 16 == 0 and rl % min(rl, FFN_CHUNK) == 0
    row_map = lambda e, m: (e, m, 0)
    wspec = lambda r, c: pl.BlockSpec((1, 1, r, c), lambda e, m: (l, e, 0, 0))
    return pl.pallas_call(
        _ffn_kernel,
        grid=(N_EXPERTS, FFN_SPLIT),
        in_specs=[pl.BlockSpec((1, rc, D), row_map), pl.BlockSpec((1, rl, D), row_map),
                  pl.BlockSpec((1, rc, 1), row_map), pl.BlockSpec((1, rl, 1), row_map),
                  wspec(D, EXPERT_FF), wspec(D, EXPERT_FF), wspec(EXPERT_FF, D)],
        out_specs=[pl.BlockSpec((1, rc, D), row_map), pl.BlockSpec((1, rl, D), row_map)],
        out_shape=[jax.ShapeDtypeStruct(xg_c.shape, BF16), jax.ShapeDtypeStruct(xg_l.shape, BF16)],
        scratch_shapes=[pltpu.VMEM((D, EXPERT_FF), BF16), pltpu.VMEM((D, EXPERT_FF), BF16),
                        pltpu.VMEM((EXPERT_FF, D), BF16)],
        compiler_params=_params(("arbitrary",) * 2),
        name="expert_ffn",
    )(xg_c, xg_l, gs_c, gs_l, w_gate, w_up, w_down)


SC_TM = 1024


def _scatter_kernel(x_ref, og_ref, pt_ref, g2_ref, ng_ref, o_ref, *, final):
    lane = lax.broadcasted_iota(I32, (SC_TM, SLOTS), 1)
    pt = pt_ref[...]
    onehot = jnp.concatenate(
        [jnp.where(lane == pt[:, ex:ex + 1], 1.0, 0.0).astype(BF16) for ex in range(N_EXPERTS)], axis=1)
    upd = _dot(onehot, og_ref[...].reshape(N_EXPERTS * SLOTS, D))
    x = x_ref[...] + g2_ref[0] * upd
    if final:
        x = x * lax.rsqrt(jnp.mean(x * x, axis=-1, keepdims=True) + EPS) * ng_ref[...]
    o_ref[...] = x


def _scatter(x, outg, pos_t, mods, norm_final, l, *, row_of_tile, final):
    t = x.shape[0]
    per = TOK_TILE // SC_TM
    return pl.pallas_call(
        functools.partial(_scatter_kernel, final=final),
        grid=(t // SC_TM,),
        in_specs=[pl.BlockSpec((SC_TM, D), lambda i: (i, 0)),
                  pl.BlockSpec((N_EXPERTS, SLOTS, D), lambda i: (0, i // per, 0)),
                  pl.BlockSpec((SC_TM, N_EXPERTS), lambda i: (i, 0)),
                  _mod_spec(l, 5, row_of_tile, 1),
                  pl.BlockSpec((1, D), lambda i: (0, 0))],
        out_specs=pl.BlockSpec((SC_TM, D), lambda i: (i, 0)),
        out_shape=jax.ShapeDtypeStruct((t, D), F32),
        compiler_params=_params(("arbitrary",)),
        name="scatter_residual",
    )(x, outg, pos_t, mods, norm_final)


def kernel(x_prompt, x_sample, cache_a_k, cache_a_v, cache_b_k, cache_b_v, state_c_fwd, state_c_bwd, c, c_ctx, w_mod, b_mod, norm_mix, norm_ffn, w_in, diff_lambda, diff_norm, na_rpb, ret_decay, ret_norm, w_proj_a, w_proj_b, w_proj_c, w_out, w_router, w_exp_gate, w_exp_up, w_exp_down, norm_final):
    nb_c, n_c, _ = x_prompt.shape
    nb_l, n_l, _ = x_sample.shape
    past = cache_a_k.shape[2]
    assert (n_c, n_l, past) == (256, TOK_TILE, 256) and nb_c % 8 == 0 and nb_l + 1 <= 16

    cvec = jnp.zeros((16, D), F32).at[0].set(c_ctx).at[1:1 + nb_l].set(c)
    mods = _modulation(cvec, w_mod, b_mod).reshape(DEPTH * 16, 1, 6 * D)
    tables = _na_tables(na_rpb)
    rope_tabs = _rope_tables(n_l)

    norm_mix3 = norm_mix.reshape(DEPTH, 1, D)
    norm_ffn3 = norm_ffn.reshape(DEPTH, 1, D)
    diff_norm3 = diff_norm.reshape(DEPTH, 1, 512)
    diff_norm_col = diff_norm.reshape(DEPTH, 512, 1)
    ret_norm3 = ret_norm.reshape(DEPTH, 1, 512)
    w_router_t = jnp.swapaxes(w_router, 1, 2)
    norm_final2 = norm_final.reshape(1, D)
    ca_k = cache_a_k.reshape(nb_l, DEPTH, past, 512)
    ca_v = cache_a_v.reshape(nb_l, DEPTH, past, 512)
    cb_k = cache_b_k.reshape(nb_l, DEPTH, past, 512)
    cb_v = cache_b_v.reshape(nb_l, DEPTH, past, 512)

    ctx_row = lambda i: 0
    lat_rows = {tm: (lambda i, per=n_l // tm: 1 + i // per) for tm in (512, SC_TM, TOK_TILE)}

    xc = x_prompt.reshape(nb_c * n_c, D)
    xl = x_sample.reshape(nb_l * n_l, D)
    caches = None
    states = None
    for l in range(DEPTH):
        last = l == DEPTH - 1
        outs = _project(xc, mods, norm_mix3, w_in, l, tm=1024, row_of_tile=ctx_row, caches=caches, n_ctx=nb_c)
        zc, gc, caches = outs[0], outs[1], outs[2:]
        zl, gl = _project(xl, mods, norm_mix3, w_in, l, tm=TOK_TILE, row_of_tile=lat_rows[TOK_TILE],
                          rope_tabs=rope_tabs)
        zc3 = zc.reshape(nb_c, n_c, Z_W)
        zl3 = zl.reshape(nb_l, n_l, Z_W)
        a_c, b_c = _ctx_attention(zc3, diff_lambda, diff_norm3, l)
        a_l = _diff_attention_lat(zl3, ca_k, ca_v, diff_lambda, diff_norm_col, l, tq=512)
        b_l = _na_attention(zl3, cb_k, cb_v, tables, l)
        rc = _retention(zc3, ret_decay, ret_norm3, l, states=states, emit_state=True)
        c_c, states = rc[0], rc[1:]
        c_l = _retention(zl3, ret_decay, ret_norm3, l, s0f=state_c_fwd, s0b=state_c_bwd)[0]
        xc, hc, lgc = _merge(xc, a_c.reshape(-1, 512), b_c.reshape(-1, 512), c_c.reshape(-1, 512), gc, mods,
                             norm_ffn3, w_proj_a, w_proj_b, w_proj_c, w_out, w_router_t, l,
                             tm=512, row_of_tile=ctx_row)
        xl, hl, lgl = _merge(xl, a_l.reshape(-1, 512), b_l.reshape(-1, 512), c_l.reshape(-1, 512), gl, mods,
                             norm_ffn3, w_proj_a, w_proj_b, w_proj_c, w_out, w_router_t, l,
                             tm=512, row_of_tile=lat_rows[512])
        xg_c, gs_c, pos_c = _route(lgc, hc, nseg=TOK_TILE // n_c)
        xg_l, gs_l, pos_l = _route(lgl, hl, nseg=1)
        og_c, og_l = _expert_ffn(xg_c, xg_l, gs_c, gs_l, w_exp_gate, w_exp_up, w_exp_down, l)
        xc = _scatter(xc, og_c, pos_c.T, mods, norm_final2, l, row_of_tile=ctx_row, final=last)
        xl = _scatter(xl, og_l, pos_l.T, mods, norm_final2, l, row_of_tile=lat_rows[SC_TM], final=last)

    new_ak, new_av, new_bk, new_bv = caches
    return (xc.reshape(nb_c, n_c, D), xl.reshape(nb_l, n_l, D),
            new_ak.reshape(nb_c, DEPTH, n_c, 4, 2, A_DH), new_av.reshape(nb_c, DEPTH, n_c, 4, 128),
            new_bk.reshape(nb_c, DEPTH, n_c, 8, 64), new_bv.reshape(nb_c, DEPTH, n_c, 8, 64),
            states[0], states[1])
```

```python
import functools
import math

import numpy as np
import jax
import jax.numpy as jnp
from jax import lax
from jax.experimental import pallas as pl
from jax.experimental.pallas import tpu as pltpu

F32 = jnp.float32
BF16 = jnp.bfloat16
I32 = jnp.int32

D = 1024
DEPTH = 4
GRID_W = 64
A_DH = 64
N_EXPERTS = 16
EXPERT_FF = 1024
RET_CHUNK = 128
ROPE_BASE = 10000.0
EPS = 1e-6
NEG_INF = -1e30

Z_W = 4608
G_W = 3072
PROJ_TN = 512
PROJ_ROWS = 512
N_ZT = Z_W // PROJ_TN
N_GT = G_W // PROJ_TN
COL_AQ, COL_AK, COL_AV = 0, 4, 8
COL_BQ, COL_BK, COL_BV = 12, 16, 20
COL_CQ, COL_CK, COL_CV, COL_CG = 24, 26, 28, 32

NA_QR = 4
NA_KR = 12
NA_HALF = 4
NA_ROWS = 8
NA_COLS = 16

TOK_TILE = 2048
SLOTS = 256
VMEM_LIMIT = 56 * 1024 * 1024


def _params(sem):
    return pltpu.CompilerParams(dimension_semantics=sem, vmem_limit_bytes=VMEM_LIMIT)


def _dot(a, b):
    return jnp.dot(a, b, preferred_element_type=F32)


def _dot_nt(a, b):
    return lax.dot_general(a, b, (((1,), (1,)), ((), ())), preferred_element_type=F32)


def _half_masks(dtype, scale=1.0):
    lane = lax.broadcasted_iota(I32, (1, 128), 1)
    lo = jnp.where(lane < 64, scale, 0.0).astype(dtype)
    hi = jnp.where(lane >= 64, scale, 0.0).astype(dtype)
    return lo, hi


def _mod_kernel(c_ref, w_ref, b_ref, o_ref):
    c = c_ref[...]
    s = (c * jax.nn.sigmoid(c)).astype(BF16)
    o_ref[0] = _dot(s, w_ref[0].astype(BF16)) + b_ref[0]


def _modulation(cvec, w_mod, b_mod):
    tn = 1536
    return pl.pallas_call(
        _mod_kernel,
        grid=(DEPTH, 6 * D // tn),
        in_specs=[pl.BlockSpec((16, D), lambda l, j: (0, 0)),
                  pl.BlockSpec((1, D, tn), lambda l, j: (l, 0, j)),
                  pl.BlockSpec((1, 1, tn), lambda l, j: (l, 0, j))],
        out_specs=pl.BlockSpec((1, 16, tn), lambda l, j: (l, 0, j)),
        out_shape=jax.ShapeDtypeStruct((DEPTH, 16, 6 * D), F32),
        compiler_params=_params(("arbitrary", "arbitrary")),
        name="modulation",
    )(cvec, w_mod, b_mod.reshape(DEPTH, 1, 6 * D))


def _mod_spec(l, which, row_of_tile, ngrid):
    if ngrid == 1:
        return pl.BlockSpec((1, 1, D), lambda i: (l * 16 + row_of_tile(i), 0, which))
    return pl.BlockSpec((1, 1, D), lambda i, j: (l * 16 + row_of_tile(i), 0, which))


def _rope_tables(n):
    half = A_DH // 2
    quarter = half // 2
    freqs = ROPE_BASE ** (-np.arange(quarter, dtype=np.float64) * 2.0 / half)
    t = np.arange(n)
    row = (t // GRID_W).astype(np.float64)[:, None] * freqs
    col = (t % GRID_W).astype(np.float64)[:, None] * freqs
    ang = np.concatenate([row, row, col, col], axis=-1)
    ang = np.concatenate([ang, ang], axis=-1)
    first = (np.arange(128) % 32) < 16
    cos = np.cos(ang)
    sin = np.where(first[None, :], -np.sin(ang), np.sin(ang))
    return jnp.asarray(cos, F32), jnp.asarray(sin, F32)


def _proj_kernel(*refs, tm, rope, emit_cache, weights_outer):
    x_ref, sh_ref, sc_ref, g_ref, w_ref = refs[:5]
    k = 5
    if rope:
        cos_ref, sin_ref = refs[k:k + 2]
        k += 2
    z_ref, gt_ref = refs[k:k + 2]
    k += 2
    if emit_cache:
        cache_refs = refs[k:k + 4]
        k += 4
    h_scr = refs[k]
    if weights_outer:
        j = pl.program_id(0)
        base = pl.multiple_of(pl.program_id(1) * tm, tm)
    else:
        j = pl.program_id(1)
        base = 0

    @pl.when(j == 0)
    def _():
        x = x_ref[...]
        y = x * lax.rsqrt(jnp.mean(x * x, axis=-1, keepdims=True) + EPS) * g_ref[0]
        h_scr[pl.ds(base, tm), :] = (y * (1.0 + sc_ref[0]) + sh_ref[0]).astype(BF16)

    wb = w_ref[0].astype(BF16)
    rows = min(tm, PROJ_ROWS)

    def project(write):
        for r0 in range(0, tm, rows):
            write(r0, _dot(h_scr[pl.ds(base + r0, rows), :], wb))

    def plain(ref):
        def write(r0, z):
            ref[r0:r0 + rows, :] = z.astype(BF16)
        return write

    def rotated(r0, z):
        lane = lax.broadcasted_iota(I32, (rows, 128), 1)
        first = (lane & 31) < 16
        cos = cos_ref[r0:r0 + rows, :]
        sin = sin_ref[r0:r0 + rows, :]
        for cc in range(PROJ_TN // 128):
            zc = z[:, cc * 128:(cc + 1) * 128]
            rot = jnp.where(first, pltpu.roll(zc, 112, 1), pltpu.roll(zc, 16, 1))
            z_ref[r0:r0 + rows, cc * 128:(cc + 1) * 128] = (zc * cos + rot * sin).astype(BF16)

    def cached(ref):
        def write(r0, z):
            z_ref[r0:r0 + rows, :] = z.astype(BF16)
            ref[r0 // 256:(r0 + rows) // 256] = z.reshape(rows // 256, 1, 256, PROJ_TN)
        return write

    special = ()
    if rope:
        special = (0, 1)
        pl.when(j < 2)(lambda: project(rotated))
    if emit_cache:
        special = (1, 2, 4, 5)
        for tile, ref in zip(special, cache_refs):
            pl.when(j == tile)(functools.partial(project, cached(ref)))
    is_special = functools.reduce(jnp.logical_or, [j == t for t in special])
    pl.when((j < N_ZT) & jnp.logical_not(is_special))(lambda: project(plain(z_ref)))
    pl.when(j >= N_ZT)(lambda: project(plain(gt_ref)))


def _project(x, mods, norm_g, w_in, l, *, tm, row_of_tile, rope_tabs=None, caches=None, n_ctx=None):
    t = x.shape[0]
    nt = t // tm
    rope = rope_tabs is not None
    emit_cache = n_ctx is not None
    weights_outer = emit_cache
    ncol = N_ZT + N_GT
    if weights_outer:
        def span(first, last):
            def rows_of(j, i):
                return jnp.where(j < first, 0, jnp.where(j <= last, i, nt - 1))
            return rows_of
        grid = (ncol, nt)
        x_map = lambda j, i: (jnp.where(j == 0, i, nt - 1), 0)
        z_map = lambda j, i: (span(0, N_ZT - 1)(j, i), jnp.minimum(j, N_ZT - 1))
        g_map = lambda j, i: (span(N_ZT, ncol - 1)(j, i), jnp.maximum(j - N_ZT, 0))
        cache_maps = [lambda j, i, rows_of=span(tile, tile): (rows_of(j, i), l, 0, 0) for tile in (1, 2, 4, 5)]
        w_map = lambda j, i: (l, 0, j)
        const = lambda j, i: (l, 0, 0)
        mod_specs = [pl.BlockSpec((1, 1, D), lambda j, i, which=which: (l * 16 + row_of_tile(i), 0, which))
                     for which in (0, 1)]
    else:
        grid = (nt, ncol)
        x_map = lambda i, j: (i, 0)
        z_map = lambda i, j: (i, jnp.minimum(j, N_ZT - 1))
        g_map = lambda i, j: (i, jnp.maximum(j - N_ZT, 0))
        cache_maps = [lambda i, j: (i, l, 0, 0)] * 4
        w_map = lambda i, j: (l, 0, j)
        const = lambda i, j: (l, 0, 0)
        mod_specs = [_mod_spec(l, 0, row_of_tile, 2), _mod_spec(l, 1, row_of_tile, 2)]
    in_specs = [pl.BlockSpec((tm, D), x_map), *mod_specs,
                pl.BlockSpec((1, 1, D), const),
                pl.BlockSpec((1, D, PROJ_TN), w_map)]
    args = [x, mods, mods, norm_g, w_in]
    if rope:
        in_specs += [pl.BlockSpec((tm, 128), lambda i, j: (0, 0))] * 2
        args += list(rope_tabs)
    out_specs = [pl.BlockSpec((tm, PROJ_TN), z_map), pl.BlockSpec((tm, PROJ_TN), g_map)]
    out_shape = [jax.ShapeDtypeStruct((t, Z_W), BF16), jax.ShapeDtypeStruct((t, G_W), BF16)]
    aliases = {}
    if emit_cache:
        nb = tm // 256
        out_specs += [pl.BlockSpec((nb, 1, 256, PROJ_TN), m) for m in cache_maps]
        out_shape += [jax.ShapeDtypeStruct((n_ctx, DEPTH, 256, PROJ_TN), F32)] * 4
        if caches is not None:
            base = len(args)
            in_specs += [pl.BlockSpec(memory_space=pl.ANY)] * 4
            args += list(caches)
            aliases = {base + a: 2 + a for a in range(4)}
    kw = dict(tm=tm, rope=rope, emit_cache=emit_cache, weights_outer=weights_outer)
    kern = functools.partial(_proj_kernel, **kw)
    if emit_cache and caches is not None:
        def kern(*refs):
            n_in = len(args)
            return _proj_kernel(*refs[:n_in - 4], *refs[n_in:], **kw)
    return pl.pallas_call(
        kern,
        grid=grid,
        in_specs=in_specs, out_specs=out_specs, out_shape=out_shape,
        scratch_shapes=[pltpu.VMEM((t if weights_outer else tm, D), BF16)],
        input_output_aliases=aliases,
        compiler_params=_params(("arbitrary", "arbitrary")),
        name="in_proj_ctx" if emit_cache else "in_proj_lat",
    )(*args)


ONES_ROWS = 16

def _diff_lambda(dl_ref, lam_init):
    dl = dl_ref[0]
    la = jnp.sum(dl[0:1] * dl[1:2], axis=-1, keepdims=True)
    lb = jnp.sum(dl[2:3] * dl[3:4], axis=-1, keepdims=True)
    return jnp.exp(la) - jnp.exp(lb) + lam_init


def _diff_lat_kernel(q_ref, k_ref, v_ref, ck_ref, cv_ref, dl_ref, gain_ref, o_ref, vt_s, *, n, lam_init):
    @pl.when(pl.program_id(2) == 0)
    def _():
        vt_s[:, :n] = v_ref[0].astype(F32).T.astype(BF16)
        vt_s[:, n:] = cv_ref[0, 0].T.astype(BF16)

    lo, hi = _half_masks(BF16, A_DH ** -0.5)
    kk = k_ref[0]
    ck = ck_ref[0, 0].astype(BF16)
    lam = _diff_lambda(dl_ref, lam_init)
    q = q_ref[0]
    es, dens = [], []
    for msk in (lo, hi):
        qh = q * msk
        s = _dot_nt(kk, qh)
        s2 = _dot_nt(ck, qh)
        m = jnp.maximum(jnp.max(s, axis=0, keepdims=True), jnp.max(s2, axis=0, keepdims=True))
        e = jnp.exp(s - m)
        e2 = jnp.exp(s2 - m)
        es.append((e, e2))
        dens.append(jnp.sum(e, axis=0, keepdims=True) + jnp.sum(e2, axis=0, keepdims=True))
    r = lam * dens[0] / dens[1]
    a = (es[0][0] - r * es[1][0]).astype(BF16)
    a2 = (es[0][1] - r * es[1][1]).astype(BF16)
    ot = (_dot(vt_s[:, :n], a) + _dot(vt_s[:, n:], a2)) / dens[0]
    on = ot * lax.rsqrt(jnp.mean(ot * ot, axis=0, keepdims=True) + EPS)
    o_ref[0] = ((on * gain_ref[0]) * (1.0 - lam_init)).T.astype(BF16)


def _diff_attention_lat(z3, cache_k, cache_v, diff_lambda, diff_norm_col, l, *, tq):
    b, n, _ = z3.shape
    past = cache_k.shape[2]
    lam_init = 0.8 - 0.6 * math.exp(-0.3 * l)
    return pl.pallas_call(
        functools.partial(_diff_lat_kernel, n=n, lam_init=lam_init),
        grid=(b, 4, n // tq),
        in_specs=[pl.BlockSpec((1, tq, 128), lambda bi, h, qi: (bi, qi, COL_AQ + h)),
                  pl.BlockSpec((1, n, 128), lambda bi, h, qi: (bi, 0, COL_AK + h)),
                  pl.BlockSpec((1, n, 128), lambda bi, h, qi: (bi, 0, COL_AV + h)),
                  pl.BlockSpec((1, 1, past, 128), lambda bi, h, qi: (bi, l, 0, h)),
                  pl.BlockSpec((1, 1, past, 128), lambda bi, h, qi: (bi, l, 0, h)),
                  pl.BlockSpec((1, 4, A_DH), lambda bi, h, qi: (l, 0, 0)),
                  pl.BlockSpec((1, 128, 1), lambda bi, h, qi: (l, h, 0))],
        out_specs=pl.BlockSpec((1, tq, 128), lambda bi, h, qi: (bi, qi, h)),
        out_shape=jax.ShapeDtypeStruct((b, n, 512), BF16),
        scratch_shapes=[pltpu.VMEM((128, n + past), BF16)],
        compiler_params=_params(("arbitrary",) * 3),
        name="diff_attn_lat",
    )(z3, z3, z3, cache_k, cache_v, diff_lambda, diff_norm_col)


def _ctx_attn_kernel(aq_ref, ak_ref, av_ref, bq_ref, bk_ref, bv_ref, dl_ref, gain_ref, ao_ref, bo_ref,
                     *, lam_init):
    lo, hi = _half_masks(BF16, A_DH ** -0.5)
    one_lo, one_hi = _half_masks(BF16)
    lam = _diff_lambda(dl_ref, lam_init)
    for h in range(4):
        hs = slice(h * 128, h * 128 + 128)
        q = aq_ref[0, :, hs]
        kk = ak_ref[0, :, hs]
        v = av_ref[0, :, hs]
        outs = []
        for msk in (lo, hi):
            s = _dot_nt(q * msk, kk)
            e = jnp.exp(s - jnp.max(s, axis=-1, keepdims=True))
            outs.append(_dot(e.astype(BF16), v) / jnp.sum(e, axis=-1, keepdims=True))
        o = outs[0] - lam * outs[1]
        on = o * lax.rsqrt(jnp.mean(o * o, axis=-1, keepdims=True) + EPS)
        ao_ref[0, :, hs] = ((on * gain_ref[0, :, hs]) * (1.0 - lam_init)).astype(BF16)

        q = bq_ref[0, :, hs]
        kk = bk_ref[0, :, hs]
        v = bv_ref[0, :, hs]
        acc = None
        for msk, vmsk in ((lo, one_lo), (hi, one_hi)):
            s = _dot_nt(q * msk, kk)
            e = jnp.exp(s - jnp.max(s, axis=-1, keepdims=True))
            o = _dot(e.astype(BF16), v * vmsk) / jnp.sum(e, axis=-1, keepdims=True)
            acc = o if acc is None else acc + o
        bo_ref[0, :, hs] = acc.astype(BF16)


def _ctx_attention(z3, diff_lambda, diff_norm, l):
    b, n, _ = z3.shape
    lam_init = 0.8 - 0.6 * math.exp(-0.3 * l)
    col = lambda c: pl.BlockSpec((1, n, 512), lambda bi: (bi, 0, c))
    return pl.pallas_call(
        functools.partial(_ctx_attn_kernel, lam_init=lam_init),
        grid=(b,),
        in_specs=[col(0), col(1), col(2), col(3), col(4), col(5),
                  pl.BlockSpec((1, 4, A_DH), lambda bi: (l, 0, 0)),
                  pl.BlockSpec((1, 1, 512), lambda bi: (l, 0, 0))],
        out_specs=[pl.BlockSpec((1, n, 512), lambda bi: (bi, 0, 0))] * 2,
        out_shape=[jax.ShapeDtypeStruct((b, n, 512), BF16)] * 2,
        compiler_params=_params(("arbitrary",)),
        name="attn_ctx",
    )(z3, z3, z3, z3, z3, z3, diff_lambda, diff_norm)


def _na_table_kernel(rpb_ref, o_ref):
    l = pl.program_id(0)
    h = pl.program_id(1)
    nrel_r = 2 * NA_ROWS - 1
    nrel_c = 2 * NA_COLS - 1
    base = (l * 8 + h) * (nrel_r * nrel_c)
    kc = lax.broadcasted_iota(I32, (GRID_W, GRID_W), 0)
    qc = lax.broadcasted_iota(I32, (GRID_W, GRID_W), 1)
    dcm = jnp.clip(kc - qc + (NA_COLS - 1), 0, nrel_c - 1)
    ws = jnp.clip(qc - NA_COLS // 2, 0, GRID_W - NA_COLS)
    col_ok = (kc >= ws) & (kc < ws + NA_COLS)
    neg = jnp.full((GRID_W, GRID_W), NEG_INF, F32)
    tabs = []
    for dr in range(nrel_r):
        acc = jnp.zeros((GRID_W, GRID_W), F32)
        for dc in range(nrel_c):
            acc = jnp.where(dcm == dc, rpb_ref[base + dr * nrel_c + dc], acc)
        tabs.append(jnp.where(col_ok, acc, neg))
    for typ in range(3):
        for i in range(NA_QR):
            for j in range(NA_KR):
                if typ == 0:
                    valid, dr = j < NA_ROWS, j - i + 7
                elif typ == 1:
                    valid, dr = i <= j < i + NA_ROWS, j - i + 3
                else:
                    valid, dr = j >= 4, j - i - 1
                o_ref[0, 0, typ, j * 64:(j + 1) * 64, i * 64:(i + 1) * 64] = tabs[dr] if valid else neg


def _na_tables(na_rpb):
    return pl.pallas_call(
        _na_table_kernel,
        grid=(DEPTH, 8),
        in_specs=[pl.BlockSpec(memory_space=pltpu.SMEM)],
        out_specs=pl.BlockSpec((1, 1, 3, NA_KR * 64, NA_QR * 64), lambda l, h: (l, h, 0, 0, 0)),
        out_shape=jax.ShapeDtypeStruct((DEPTH, 8, 3, NA_KR * 64, NA_QR * 64), F32),
        compiler_params=_params(("arbitrary",) * 2),
        name="na_bias_tables",
    )(na_rpb.reshape(-1))


def _na_kernel(q_ref, k_ref, v_ref, ck_ref, cv_ref, tab_ref, o_ref, vt_s, cvt_s, slot_a, slot_b, *, n, nrb):
    rb = pl.program_id(2)
    nwin = NA_KR * GRID_W

    @pl.when(rb == 0)
    def _():
        for pr in range(2):
            ps = slice(pr * 128, pr * 128 + 128)
            vt = v_ref[0, :, ps].astype(F32).T
            for c in range(n // 128):
                vt_s[pr, c] = vt[:, c * 128:(c + 1) * 128].astype(BF16)
            cvt_s[pr] = cv_ref[0, 0, :, ps].T.astype(BF16)

    def first_key_row(block):
        return jnp.clip(block * NA_QR - NA_HALF, 0, 32 - NA_KR)

    def scores(dst):
        typ = jnp.where(rb == 0, 0, jnp.where(rb == nrb - 1, 2, 1))
        off = pl.multiple_of(first_key_row(rb) * GRID_W, 256)
        lo, hi = _half_masks(BF16, 0.125)
        for pr in range(2):
            ps = slice(pr * 128, pr * 128 + 128)
            q = q_ref[0, :, ps]
            kw = k_ref[0, pl.ds(off, nwin), ps]
            ck = ck_ref[0, 0, :, ps].astype(BF16)
            for half, msk in enumerate((lo, hi)):
                qh = q * msk
                dst[pr * 2 + half, :nwin, :] = _dot_nt(kw, qh) + tab_ref[0, pr * 2 + half, typ]
                dst[pr * 2 + half, nwin:, :] = _dot_nt(ck, qh)

    def finish(src):
        c0 = first_key_row(rb - 1) // 2
        for pr in range(2):
            vtw = jnp.concatenate([vt_s[pr, c0 + c] for c in range(nwin // 128)] + [cvt_s[pr]], axis=1)
            outs = []
            for half in range(2):
                s = src[pr * 2 + half]
                e = jnp.exp(s - jnp.max(s, axis=0, keepdims=True)).astype(BF16)
                lhs = jnp.concatenate([vtw[half * 64:half * 64 + 64],
                                       jnp.ones((ONES_ROWS, vtw.shape[1]), BF16)], axis=0)
                acc = _dot(lhs, e)
                outs.append(acc[:64] / acc[64:65])
            o_ref[0, :, pr * 128:pr * 128 + 128] = jnp.concatenate(outs, axis=0).T.astype(BF16)

    even = rb % 2 == 0

    @pl.when(rb == 0)
    def _():
        scores(slot_a)

    @pl.when((rb > 0) & (rb < nrb) & even)
    def _():
        scores(slot_a)
        finish(slot_b)

    @pl.when((rb < nrb) & jnp.logical_not(even))
    def _():
        scores(slot_b)
        finish(slot_a)

    @pl.when(rb == nrb)
    def _():
        finish(slot_b if nrb % 2 == 0 else slot_a)


def _na_attention(z3, cache_k, cache_v, tables, l):
    b, n, _ = z3.shape
    past = cache_k.shape[2]
    tq = NA_QR * GRID_W
    nrb = n // tq
    slot = pltpu.VMEM((4, NA_KR * GRID_W + past, tq), F32)
    return pl.pallas_call(
        functools.partial(_na_kernel, n=n, nrb=nrb),
        grid=(2, b, nrb + 1),
        in_specs=[pl.BlockSpec((1, tq, 256), lambda pp, bi, rb: (bi, jnp.minimum(rb, nrb - 1), COL_BQ // 2 + pp)),
                  pl.BlockSpec((1, n, 256), lambda pp, bi, rb: (bi, 0, COL_BK // 2 + pp)),
                  pl.BlockSpec((1, n, 256), lambda pp, bi, rb: (bi, 0, COL_BV // 2 + pp)),
                  pl.BlockSpec((1, 1, past, 256), lambda pp, bi, rb: (bi, l, 0, pp)),
                  pl.BlockSpec((1, 1, past, 256), lambda pp, bi, rb: (bi, l, 0, pp)),
                  pl.BlockSpec((1, 4, 3, NA_KR * 64, tq), lambda pp, bi, rb: (l, pp, 0, 0, 0))],
        out_specs=pl.BlockSpec((1, tq, 256), lambda pp, bi, rb: (bi, jnp.maximum(rb - 1, 0), pp)),
        out_shape=jax.ShapeDtypeStruct((b, n, 512), BF16),
        scratch_shapes=[pltpu.VMEM((2, n // 128, 128, 128), BF16), pltpu.VMEM((2, 128, past), BF16),
                        slot, slot],
        compiler_params=_params(("arbitrary",) * 3),
        name="na_attn_lat",
    )(z3, z3, z3, cache_k, cache_v, tables)


RET_UNROLL = 16


def _ret_kernel(*refs, **kw):
    for h in range(4):
        _ret_head(*refs, h=h, **kw)


def _ret_head(*refs, h, n, l, has_state, emit_state):
    rd_ref, q_ref, k_ref, v_ref, g_ref, gain_ref = refs[:6]
    i = 6
    if has_state:
        s0f_ref, s0b_ref = refs[i:i + 2]
        i += 2
    o_ref = refs[i]
    i += 1
    if emit_state:
        sf_ref, sb_ref = refs[i:i + 2]
        i += 2
    uf, ub = refs[i:i + 2]
    t = RET_CHUNK
    nc = n // t
    unroll = min(nc, RET_UNROLL)
    half = h % 2
    pair = slice((h // 2) * 128, (h // 2) * 128 + 128)
    head = slice(h * 128, h * 128 + 128)

    def log_gamma(direction):
        x = jnp.full((1, 1), rd_ref[l * 8 + direction * 4 + h], F32)
        return jnp.minimum(x, 0.0) - jnp.log1p(jnp.exp(-jnp.abs(x)))

    lgf = log_gamma(0)
    lgb = log_gamma(1)
    lane = lax.broadcasted_iota(I32, (1, 128), 1)
    hmask = jnp.where((lane >= 64) if half else (lane < 64), 1.0, 0.0)
    ri = lax.broadcasted_iota(I32, (t, t), 0).astype(F32)
    ci = lax.broadcasted_iota(I32, (t, t), 1).astype(F32)
    diff = ri - ci
    dmat = jnp.where(diff > 0, jnp.exp(lgf * jnp.maximum(diff, 0.0)),
                     jnp.where(diff < 0, jnp.exp(lgb * jnp.maximum(-diff, 0.0)), 2.0))
    pos_r = lax.broadcasted_iota(I32, (1, t), 1).astype(F32)
    pos_c = lax.broadcasted_iota(I32, (t, 1), 0).astype(F32)
    upd_f = jnp.exp(lgf * (t - 1.0 - pos_r))
    upd_b = jnp.exp(lgb * pos_r)
    cross_f = jnp.exp(lgf * (pos_c + 1.0))
    cross_b = jnp.exp(lgb * (t - pos_c))
    gch_f = jnp.exp(lgf * t)
    gch_b = jnp.exp(lgb * t)
    kscale = hmask * (64 ** -0.5)

    def chunk(ref, c, lanes):
        return ref[0, pl.ds(pl.multiple_of(c * t, t), t), lanes]

    def summarise(c, carry):
        kt = (chunk(k_ref, c, pair).astype(F32) * kscale).T
        vc = chunk(v_ref, c, head)
        uf[c] = _dot((kt * upd_f).astype(BF16), vc)
        ub[c] = _dot((kt * upd_b).astype(BF16), vc)
        return carry
    lax.fori_loop(0, nc, summarise, 0, unroll=unroll)

    if has_state:
        s0f = jnp.concatenate([s0f_ref[0, 0, h], s0f_ref[0, 0, h]], axis=0)
        s0b = jnp.concatenate([s0b_ref[0, 0, h], s0b_ref[0, 0, h]], axis=0)
    else:
        s0f = jnp.zeros((128, 128), F32)
        s0b = s0f

    def scan_f(c, s):
        u = uf[c]
        uf[c] = s
        return gch_f * s + u
    s_fin_f = lax.fori_loop(0, nc, scan_f, s0f, unroll=True)

    def scan_b(cc, s):
        c = nc - 1 - cc
        u = ub[c]
        ub[c] = s
        return gch_b * s + u
    s_fin_b = lax.fori_loop(0, nc, scan_b, s0b, unroll=True)

    if emit_state:
        sf_ref[0, 0, h] = s_fin_f[half * 64:half * 64 + 64]
        sb_ref[0, 0, h] = s_fin_b[half * 64:half * 64 + 64]

    gain = gain_ref[0, :, head]

    def emit(c, carry):
        qc = (chunk(q_ref, c, pair).astype(F32) * hmask).astype(BF16)
        kc = (chunk(k_ref, c, pair).astype(F32) * kscale).astype(BF16)
        vc = chunk(v_ref, c, head)
        att = (_dot_nt(qc, kc) * dmat).astype(BF16)
        o = (_dot(att, vc)
             + _dot(qc, uf[c].astype(BF16)) * cross_f
             + _dot(qc, ub[c].astype(BF16)) * cross_b)
        mu = jnp.mean(o, axis=-1, keepdims=True)
        var = jnp.mean(jnp.square(o - mu), axis=-1, keepdims=True)
        on = (o - mu) * lax.rsqrt(var + EPS) * gain
        g = chunk(g_ref, c, head).astype(F32)
        o_ref[0, pl.ds(pl.multiple_of(c * t, t), t), head] = (g * jax.nn.sigmoid(g) * on).astype(BF16)
        return carry
    lax.fori_loop(0, nc, emit, 0, unroll=unroll)


def _retention(z3, ret_decay, ret_norm, l, *, s0f=None, s0b=None, states=None, emit_state=False):
    b, n, _ = z3.shape
    has_state = s0f is not None
    in_specs = [pl.BlockSpec(memory_space=pltpu.SMEM),
                pl.BlockSpec((1, n, 256), lambda bi: (bi, 0, COL_CQ // 2)),
                pl.BlockSpec((1, n, 256), lambda bi: (bi, 0, COL_CK // 2)),
                pl.BlockSpec((1, n, 512), lambda bi: (bi, 0, COL_CV // 4)),
                pl.BlockSpec((1, n, 512), lambda bi: (bi, 0, COL_CG // 4)),
                pl.BlockSpec((1, 1, 512), lambda bi: (l, 0, 0))]
    args = [ret_decay.reshape(-1), z3, z3, z3, z3, ret_norm]
    if has_state:
        in_specs += [pl.BlockSpec((1, 1, 4, 64, 128), lambda bi: (bi, l, 0, 0, 0))] * 2
        args += [s0f, s0b]
    out_specs = [pl.BlockSpec((1, n, 512), lambda bi: (bi, 0, 0))]
    out_shape = [jax.ShapeDtypeStruct((b, n, 512), BF16)]
    aliases = {}
    n_in = len(args)
    if emit_state:
        out_specs += [pl.BlockSpec((1, 1, 4, 64, 128), lambda bi: (bi, l, 0, 0, 0))] * 2
        out_shape += [jax.ShapeDtypeStruct((b, DEPTH, 4, 64, 128), F32)] * 2
        if states is not None:
            in_specs += [pl.BlockSpec(memory_space=pl.ANY)] * 2
            args += list(states)
            aliases = {n_in: 1, n_in + 1: 2}
    kw = dict(n=n, l=l, has_state=has_state, emit_state=emit_state)

    def kern(*refs):
        return _ret_kernel(*refs[:n_in], *refs[len(args):], **kw)
    return pl.pallas_call(
        kern,
        grid=(b,),
        in_specs=in_specs, out_specs=out_specs, out_shape=out_shape,
        scratch_shapes=[pltpu.VMEM((n // RET_CHUNK, 128, 128), F32)] * 2,
        input_output_aliases=aliases,
        compiler_params=_params(("arbitrary",)),
        name="retention_ctx" if emit_state else "retention_lat",
    )(*args)


def _merge_kernel(x_ref, a_ref, b_ref, c_ref, gt_ref, wa_ref, wb_ref, wc_ref, wo_ref,
                  g1_ref, sh_ref, sc_ref, ng_ref, wr_ref, xo_ref, h_ref, lg_ref,
                  wa_s, wb_s, wc_s, wo_s, wr_s):
    @pl.when(pl.program_id(0) == 0)
    def _():
        wa_s[...] = wa_ref[0].astype(BF16)
        wb_s[...] = wb_ref[0].astype(BF16)
        wc_s[...] = wc_ref[0].astype(BF16)
        wo_s[...] = wo_ref[0].astype(BF16)
        wr_s[...] = wr_ref[0].astype(BF16)

    merged = None
    for br, (o_ref, w_s) in enumerate(((a_ref, wa_s), (b_ref, wb_s), (c_ref, wc_s))):
        gate = jax.nn.sigmoid(gt_ref[:, br * D:(br + 1) * D].astype(F32))
        term = gate * _dot(o_ref[...], w_s[...])
        merged = term if merged is None else merged + term
    x = x_ref[...] + g1_ref[0] * _dot(merged.astype(BF16), wo_s[...])
    xo_ref[...] = x
    y = x * lax.rsqrt(jnp.mean(x * x, axis=-1, keepdims=True) + EPS) * ng_ref[0]
    h = (y * (1.0 + sc_ref[0]) + sh_ref[0]).astype(BF16)
    h_ref[...] = h
    lg_ref[...] = _dot_nt(wr_s[...], h)


def _merge(x, a_o, b_o, c_o, gates, mods, norm_ffn, w_pa, w_pb, w_pc, w_out, w_router_t, l, *, tm, row_of_tile):
    t = x.shape[0]
    tok = lambda w: pl.BlockSpec((tm, w), lambda i: (i, 0))
    wspec = lambda r, c: pl.BlockSpec((1, r, c), lambda i: (l, 0, 0))
    return pl.pallas_call(
        _merge_kernel,
        grid=(t // tm,),
        in_specs=[tok(D), tok(512), tok(512), tok(512), tok(G_W),
                  wspec(512, D), wspec(512, D), wspec(512, D), wspec(D, D),
                  _mod_spec(l, 2, row_of_tile, 1), _mod_spec(l, 3, row_of_tile, 1),
                  _mod_spec(l, 4, row_of_tile, 1),
                  pl.BlockSpec((1, 1, D), lambda i: (l, 0, 0)), wspec(N_EXPERTS, D)],
        out_specs=[tok(D), tok(D), pl.BlockSpec((N_EXPERTS, tm), lambda i: (0, i))],
        out_shape=[jax.ShapeDtypeStruct((t, D), F32), jax.ShapeDtypeStruct((t, D), BF16),
                   jax.ShapeDtypeStruct((N_EXPERTS, t), F32)],
        scratch_shapes=[pltpu.VMEM((512, D), BF16)] * 3 + [pltpu.VMEM((D, D), BF16),
                                                            pltpu.VMEM((N_EXPERTS, D), BF16)],
        compiler_params=_params(("arbitrary",)),
        name="merge_out_proj",
    )(x, a_o, b_o, c_o, gates, w_pa, w_pb, w_pc, w_out, mods, mods, mods, norm_ffn, w_router_t)


def _route_kernel(lg_ref, h_ref, xg_ref, gs_ref, pos_ref, aff_s, *, nseg):
    seg_len = TOK_TILE // nseg
    cap = seg_len // 8
    cps = seg_len // 256
    lg = lg_ref[...]
    e = jnp.exp(lg - jnp.max(lg, axis=0, keepdims=True))
    aff = e / jnp.sum(e, axis=0, keepdims=True)
    aff_s[...] = aff

    def seg_sums(x):
        return [jnp.sum(x[:, s * seg_len:(s + 1) * seg_len], axis=1, keepdims=True) for s in range(nseg)]

    def seg_spread(vals):
        return jnp.concatenate([jnp.broadcast_to(v, (N_EXPERTS, seg_len)) for v in vals], axis=1)

    def search(it, thr):
        cand = thr | jnp.left_shift(jnp.int32(1), 30 - it)
        ge = aff >= lax.bitcast_convert_type(cand, F32)
        cnt = seg_spread(seg_sums(jnp.where(ge, 1.0, 0.0)))
        return jnp.where(cnt >= cap, cand, thr)
    thr_bits = lax.fori_loop(0, 31, search, jnp.zeros((N_EXPERTS, TOK_TILE), I32))
    thr = lax.bitcast_convert_type(thr_bits, F32)
    thr_next = lax.bitcast_convert_type(thr_bits + 1, F32)

    r = lax.broadcasted_iota(I32, (256, 256), 0)
    c = lax.broadcasted_iota(I32, (256, 256), 1)
    tri = jnp.where(r < c, 1.0, 0.0).astype(BF16)

    def excl_cumsum(x):
        outs = []
        run = None
        for ch in range(TOK_TILE // 256):
            xc = x[:, ch * 256:(ch + 1) * 256]
            p = _dot(xc.astype(BF16), tri)
            if cps > 1:
                if ch % cps == 0:
                    run = jnp.zeros((N_EXPERTS, 1), F32)
                p = p + run
                run = run + jnp.sum(xc, axis=1, keepdims=True)
            outs.append(p)
        return jnp.concatenate(outs, axis=1)

    gt = jnp.where(aff >= thr_next, 1.0, 0.0)
    eq = jnp.where(aff >= thr, 1.0, 0.0) - gt
    need = cap - seg_spread(seg_sums(gt))
    sel = gt + eq * jnp.where(excl_cumsum(eq) < need, 1.0, 0.0)
    seg_base = seg_spread([jnp.full((N_EXPERTS, 1), float(s * cap), F32) for s in range(nseg)])
    slot = (excl_cumsum(sel) + seg_base).astype(I32)
    pos_ref[...] = jnp.where(sel > 0.5, slot, -1)

    slot_iota = lax.broadcasted_iota(I32, (SLOTS, TOK_TILE), 0)
    h = h_ref[...]
    for grp in range(N_EXPERTS // 4):
        onehots = []
        for ex in range(grp * 4, grp * 4 + 4):
            hit = slot_iota == pos_ref[ex:ex + 1, :]
            gs_ref[ex] = jnp.sum(jnp.where(hit, aff_s[ex:ex + 1, :], 0.0), axis=1, keepdims=True)
            onehots.append(jnp.where(hit, 1.0, 0.0).astype(BF16))
        xg = _dot(jnp.concatenate(onehots, axis=0), h)
        xg_ref[grp * 4:grp * 4 + 4] = xg.astype(BF16).reshape(4, SLOTS, D)


def _route(logits_t, h, *, nseg):
    t = h.shape[0]
    nt = t // TOK_TILE
    return pl.pallas_call(
        functools.partial(_route_kernel, nseg=nseg),
        grid=(nt,),
        in_specs=[pl.BlockSpec((N_EXPERTS, TOK_TILE), lambda i: (0, i)),
                  pl.BlockSpec((TOK_TILE, D), lambda i: (i, 0))],
        out_specs=[pl.BlockSpec((N_EXPERTS, SLOTS, D), lambda i: (0, i, 0)),
                   pl.BlockSpec((N_EXPERTS, SLOTS, 1), lambda i: (0, i, 0)),
                   pl.BlockSpec((N_EXPERTS, TOK_TILE), lambda i: (0, i))],
        out_shape=[jax.ShapeDtypeStruct((N_EXPERTS, nt * SLOTS, D), BF16),
                   jax.ShapeDtypeStruct((N_EXPERTS, nt * SLOTS, 1), F32),
                   jax.ShapeDtypeStruct((N_EXPERTS, t), I32)],
        scratch_shapes=[pltpu.VMEM((N_EXPERTS, TOK_TILE), F32)],
        compiler_params=_params(("arbitrary",)),
        name="route_gather",
    )(logits_t, h)


FFN_SPLIT = 2
FFN_CHUNK = 512


def _ffn_kernel(xc_ref, xl_ref, gc_ref, gl_ref, wg_ref, wu_ref, wd_ref, oc_ref, ol_ref, wg_s, wu_s, wd_s):
    @pl.when(pl.program_id(1) == 0)
    def _():
        wg_s[...] = wg_ref[0, 0].astype(BF16)
        wu_s[...] = wu_ref[0, 0].astype(BF16)
        wd_s[...] = wd_ref[0, 0].astype(BF16)

    for x_ref, g_ref, o_ref in ((xc_ref, gc_ref, oc_ref), (xl_ref, gl_ref, ol_ref)):
        rows = x_ref.shape[1]
        step = min(rows, FFN_CHUNK)
        for r0 in range(0, rows, step):
            rs = slice(r0, r0 + step)
            x = x_ref[0, rs, :]
            a = _dot(x, wg_s[...])
            hid = (a * jax.nn.sigmoid(a) * _dot(x, wu_s[...])).astype(BF16)
            o_ref[0, rs, :] = (_dot(hid, wd_s[...]) * g_ref[0, rs, :]).astype(BF16)


def _expert_ffn(xg_c, xg_l, gs_c, gs_l, w_gate, w_up, w_down, l):
    rc = xg_c.shape[1] // FFN_SPLIT
    rl = xg_l.shape[1] // FFN_SPLIT
    assert rc % 16 == 0 and rl % min(rl, FFN_CHUNK) == 0
    row_map = lambda e, m: (e, m, 0)
    wspec = lambda r, c: pl.BlockSpec((1, 1, r, c), lambda e, m: (l, e, 0, 0))
    return pl.pallas_call(
        _ffn_kernel,
        grid=(N_EXPERTS, FFN_SPLIT),
        in_specs=[pl.BlockSpec((1, rc, D), row_map), pl.BlockSpec((1, rl, D), row_map),
                  pl.BlockSpec((1, rc, 1), row_map), pl.BlockSpec((1, rl, 1), row_map),
                  wspec(D, EXPERT_FF), wspec(D, EXPERT_FF), wspec(EXPERT_FF, D)],
        out_specs=[pl.BlockSpec((1, rc, D), row_map), pl.BlockSpec((1, rl, D), row_map)],
        out_shape=[jax.ShapeDtypeStruct(xg_c.shape, BF16), jax.ShapeDtypeStruct(xg_l.shape, BF16)],
        scratch_shapes=[pltpu.VMEM((D, EXPERT_FF), BF16), pltpu.VMEM((D, EXPERT_FF), BF16),
                        pltpu.VMEM((EXPERT_FF, D), BF16)],
        compiler_params=_params(("arbitrary",) * 2),
        name="expert_ffn",
    )(xg_c, xg_l, gs_c, gs_l, w_gate, w_up, w_down)


SC_TM = 1024


def _scatter_kernel(x_ref, og_ref, pt_ref, g2_ref, ng_ref, o_ref, *, final):
    lane = lax.broadcasted_iota(I32, (SC_TM, SLOTS), 1)
    pt = pt_ref[...]
    onehot = jnp.concatenate(
        [jnp.where(lane == pt[:, ex:ex + 1], 1.0, 0.0).astype(BF16) for ex in range(N_EXPERTS)], axis=1)
    upd = _dot(onehot, og_ref[...].reshape(N_EXPERTS * SLOTS, D))
    x = x_ref[...] + g2_ref[0] * upd
    if final:
        x = x * lax.rsqrt(jnp.mean(x * x, axis=-1, keepdims=True) + EPS) * ng_ref[...]
    o_ref[...] = x


def _scatter(x, outg, pos_t, mods, norm_final, l, *, row_of_tile, final):
    t = x.shape[0]
    per = TOK_TILE // SC_TM
    return pl.pallas_call(
        functools.partial(_scatter_kernel, final=final),
        grid=(t // SC_TM,),
        in_specs=[pl.BlockSpec((SC_TM, D), lambda i: (i, 0)),
                  pl.BlockSpec((N_EXPERTS, SLOTS, D), lambda i: (0, i // per, 0)),
                  pl.BlockSpec((SC_TM, N_EXPERTS), lambda i: (i, 0)),
                  _mod_spec(l, 5, row_of_tile, 1),
                  pl.BlockSpec((1, D), lambda i: (0, 0))],
        out_specs=pl.BlockSpec((SC_TM, D), lambda i: (i, 0)),
        out_shape=jax.ShapeDtypeStruct((t, D), F32),
        compiler_params=_params(("arbitrary",)),
        name="scatter_residual",
    )(x, outg, pos_t, mods, norm_final)


def kernel(x_prompt, x_sample, cache_a_k, cache_a_v, cache_b_k, cache_b_v, state_c_fwd, state_c_bwd, c, c_ctx, w_mod, b_mod, norm_mix, norm_ffn, w_in, diff_lambda, diff_norm, na_rpb, ret_decay, ret_norm, w_proj_a, w_proj_b, w_proj_c, w_out, w_router, w_exp_gate, w_exp_up, w_exp_down, norm_final):
    nb_c, n_c, _ = x_prompt.shape
    nb_l, n_l, _ = x_sample.shape
    past = cache_a_k.shape[2]
    assert (n_c, n_l, past) == (256, TOK_TILE, 256) and nb_c % 8 == 0 and nb_l + 1 <= 16

    cvec = jnp.zeros((16, D), F32).at[0].set(c_ctx).at[1:1 + nb_l].set(c)
    mods = _modulation(cvec, w_mod, b_mod).reshape(DEPTH * 16, 1, 6 * D)
    tables = _na_tables(na_rpb)
    rope_tabs = _rope_tables(n_l)

    norm_mix3 = norm_mix.reshape(DEPTH, 1, D)
    norm_ffn3 = norm_ffn.reshape(DEPTH, 1, D)
    diff_norm3 = diff_norm.reshape(DEPTH, 1, 512)
    diff_norm_col = diff_norm.reshape(DEPTH, 512, 1)
    ret_norm3 = ret_norm.reshape(DEPTH, 1, 512)
    w_router_t = jnp.swapaxes(w_router, 1, 2)
    norm_final2 = norm_final.reshape(1, D)
    ca_k = cache_a_k.reshape(nb_l, DEPTH, past, 512)
    ca_v = cache_a_v.reshape(nb_l, DEPTH, past, 512)
    cb_k = cache_b_k.reshape(nb_l, DEPTH, past, 512)
    cb_v = cache_b_v.reshape(nb_l, DEPTH, past, 512)

    ctx_row = lambda i: 0
    lat_rows = {tm: (lambda i, per=n_l // tm: 1 + i // per) for tm in (512, SC_TM, TOK_TILE)}

    xc = x_prompt.reshape(nb_c * n_c, D)
    xl = x_sample.reshape(nb_l * n_l, D)
    caches = None
    states = None
    for l in range(DEPTH):
        last = l == DEPTH - 1
        outs = _project(xc, mods, norm_mix3, w_in, l, tm=1024, row_of_tile=ctx_row, caches=caches, n_ctx=nb_c)
        zc, gc, caches = outs[0], outs[1], outs[2:]
        zl, gl = _project(xl, mods, norm_mix3, w_in, l, tm=TOK_TILE, row_of_tile=lat_rows[TOK_TILE],
                          rope_tabs=rope_tabs)
        zc3 = zc.reshape(nb_c, n_c, Z_W)
        zl3 = zl.reshape(nb_l, n_l, Z_W)
        a_c, b_c = _ctx_attention(zc3, diff_lambda, diff_norm3, l)
        a_l = _diff_attention_lat(zl3, ca_k, ca_v, diff_lambda, diff_norm_col, l, tq=512)
        b_l = _na_attention(zl3, cb_k, cb_v, tables, l)
        rc = _retention(zc3, ret_decay, ret_norm3, l, states=states, emit_state=True)
        c_c, states = rc[0], rc[1:]
        c_l = _retention(zl3, ret_decay, ret_norm3, l, s0f=state_c_fwd, s0b=state_c_bwd)[0]
        xc, hc, lgc = _merge(xc, a_c.reshape(-1, 512), b_c.reshape(-1, 512), c_c.reshape(-1, 512), gc, mods,
                             norm_ffn3, w_proj_a, w_proj_b, w_proj_c, w_out, w_router_t, l,
                             tm=512, row_of_tile=ctx_row)
        xl, hl, lgl = _merge(xl, a_l.reshape(-1, 512), b_l.reshape(-1, 512), c_l.reshape(-1, 512), gl, mods,
                             norm_ffn3, w_proj_a, w_proj_b, w_proj_c, w_out, w_router_t, l,
                             tm=512, row_of_tile=lat_rows[512])
        xg_c, gs_c, pos_c = _route(lgc, hc, nseg=TOK_TILE // n_c)
        xg_l, gs_l, pos_l = _route(lgl, hl, nseg=1)
        og_c, og_l = _expert_ffn(xg_c, xg_l, gs_c, gs_l, w_exp_gate, w_exp_up, w_exp_down, l)
        xc = _scatter(xc, og_c, pos_c.T, mods, norm_final2, l, row_of_tile=ctx_row, final=last)
        xl = _scatter(xl, og_l, pos_l.T, mods, norm_final2, l, row_of_tile=lat_rows[SC_TM], final=last)

    new_ak, new_av, new_bk, new_bv = caches
    return (xc.reshape(nb_c, n_c, D), xl.reshape(nb_l, n_l, D),
            new_ak.reshape(nb_c, DEPTH, n_c, 4, 2, A_DH), new_av.reshape(nb_c, DEPTH, n_c, 4, 128),
            new_bk.reshape(nb_c, DEPTH, n_c, 8, 64), new_bv.reshape(nb_c, DEPTH, n_c, 8, 64),
            states[0], states[1])
```

```python
import functools
import math

import numpy as np
import jax
import jax.numpy as jnp
from jax import lax
from jax.experimental import pallas as pl
from jax.experimental.pallas import tpu as pltpu

F32 = jnp.float32
BF16 = jnp.bfloat16
I32 = jnp.int32

D = 1024
DEPTH = 4
GRID_W = 64
A_DH = 64
N_EXPERTS = 16
EXPERT_FF = 1024
RET_CHUNK = 128
ROPE_BASE = 10000.0
EPS = 1e-6
NEG_INF = -1e30

Z_W = 4608
G_W = 3072
PROJ_TN = 512
PROJ_ROWS = 512
N_ZT = Z_W // PROJ_TN
N_GT = G_W // PROJ_TN
COL_AQ, COL_AK, COL_AV = 0, 4, 8
COL_BQ, COL_BK, COL_BV = 12, 16, 20
COL_CQ, COL_CK, COL_CV, COL_CG = 24, 26, 28, 32

NA_QR = 4
NA_KR = 12
NA_HALF = 4
NA_ROWS = 8
NA_COLS = 16

TOK_TILE = 2048
SLOTS = 256
VMEM_LIMIT = 56 * 1024 * 1024


def _params(sem):
    return pltpu.CompilerParams(dimension_semantics=sem, vmem_limit_bytes=VMEM_LIMIT)


def _dot(a, b):
    return jnp.dot(a, b, preferred_element_type=F32)


def _dot_nt(a, b):
    return lax.dot_general(a, b, (((1,), (1,)), ((), ())), preferred_element_type=F32)


def _half_masks(dtype, scale=1.0):
    lane = lax.broadcasted_iota(I32, (1, 128), 1)
    lo = jnp.where(lane < 64, scale, 0.0).astype(dtype)
    hi = jnp.where(lane >= 64, scale, 0.0).astype(dtype)
    return lo, hi


def _mod_kernel(c_ref, w_ref, b_ref, o_ref):
    c = c_ref[...]
    s = (c * jax.nn.sigmoid(c)).astype(BF16)
    o_ref[0] = _dot(s, w_ref[0].astype(BF16)) + b_ref[0]


def _modulation(cvec, w_mod, b_mod):
    tn = 1536
    return pl.pallas_call(
        _mod_kernel,
        grid=(DEPTH, 6 * D // tn),
        in_specs=[pl.BlockSpec((16, D), lambda l, j: (0, 0)),
                  pl.BlockSpec((1, D, tn), lambda l, j: (l, 0, j)),
                  pl.BlockSpec((1, 1, tn), lambda l, j: (l, 0, j))],
        out_specs=pl.BlockSpec((1, 16, tn), lambda l, j: (l, 0, j)),
        out_shape=jax.ShapeDtypeStruct((DEPTH, 16, 6 * D), F32),
        compiler_params=_params(("arbitrary", "arbitrary")),
        name="modulation",
    )(cvec, w_mod, b_mod.reshape(DEPTH, 1, 6 * D))


def _mod_spec(l, which, row_of_tile, ngrid):
    if ngrid == 1:
        return pl.BlockSpec((1, 1, D), lambda i: (l * 16 + row_of_tile(i), 0, which))
    return pl.BlockSpec((1, 1, D), lambda i, j: (l * 16 + row_of_tile(i), 0, which))


def _rope_tables(n):
    half = A_DH // 2
    quarter = half // 2
    freqs = ROPE_BASE ** (-np.arange(quarter, dtype=np.float64) * 2.0 / half)
    t = np.arange(n)
    row = (t // GRID_W).astype(np.float64)[:, None] * freqs
    col = (t % GRID_W).astype(np.float64)[:, None] * freqs
    ang = np.concatenate([row, row, col, col], axis=-1)
    ang = np.concatenate([ang, ang], axis=-1)
    first = (np.arange(128) % 32) < 16
    cos = np.cos(ang)
    sin = np.where(first[None, :], -np.sin(ang), np.sin(ang))
    return jnp.asarray(cos, F32), jnp.asarray(sin, F32)


def _proj_kernel(*refs, tm, rope, emit_cache, weights_outer):
    x_ref, sh_ref, sc_ref, g_ref, w_ref = refs[:5]
    k = 5
    if rope:
        cos_ref, sin_ref = refs[k:k + 2]
        k += 2
    z_ref, gt_ref = refs[k:k + 2]
    k += 2
    if emit_cache:
        cache_refs = refs[k:k + 4]
        k += 4
    h_scr = refs[k]
    if weights_outer:
        j = pl.program_id(0)
        base = pl.multiple_of(pl.program_id(1) * tm, tm)
    else:
        j = pl.program_id(1)
        base = 0

    @pl.when(j == 0)
    def _():
        x = x_ref[...]
        y = x * lax.rsqrt(jnp.mean(x * x, axis=-1, keepdims=True) + EPS) * g_ref[0]
        h_scr[pl.ds(base, tm), :] = (y * (1.0 + sc_ref[0]) + sh_ref[0]).astype(BF16)

    wb = w_ref[0].astype(BF16)
    rows = min(tm, PROJ_ROWS)

    def project(write):
        for r0 in range(0, tm, rows):
            write(r0, _dot(h_scr[pl.ds(base + r0, rows), :], wb))

    def plain(ref):
        def write(r0, z):
            ref[r0:r0 + rows, :] = z.astype(BF16)
        return write

    def rotated(r0, z):
        lane = lax.broadcasted_iota(I32, (rows, 128), 1)
        first = (lane & 31) < 16
        cos = cos_ref[r0:r0 + rows, :]
        sin = sin_ref[r0:r0 + rows, :]
        for cc in range(PROJ_TN // 128):
            zc = z[:, cc * 128:(cc + 1) * 128]
            rot = jnp.where(first, pltpu.roll(zc, 112, 1), pltpu.roll(zc, 16, 1))
            z_ref[r0:r0 + rows, cc * 128:(cc + 1) * 128] = (zc * cos + rot * sin).astype(BF16)

    def cached(ref):
        def write(r0, z):
            z_ref[r0:r0 + rows, :] = z.astype(BF16)
            ref[r0 // 256:(r0 + rows) // 256] = z.reshape(rows // 256, 1, 256, PROJ_TN)
        return write

    special = ()
    if rope:
        special = (0, 1)
        pl.when(j < 2)(lambda: project(rotated))
    if emit_cache:
        special = (1, 2, 4, 5)
        for tile, ref in zip(special, cache_refs):
            pl.when(j == tile)(functools.partial(project, cached(ref)))
    is_special = functools.reduce(jnp.logical_or, [j == t for t in special])
    pl.when((j < N_ZT) & jnp.logical_not(is_special))(lambda: project(plain(z_ref)))
    pl.when(j >= N_ZT)(lambda: project(plain(gt_ref)))


def _project(x, mods, norm_g, w_in, l, *, tm, row_of_tile, rope_tabs=None, caches=None, n_ctx=None):
    t = x.shape[0]
    nt = t // tm
    rope = rope_tabs is not None
    emit_cache = n_ctx is not None
    weights_outer = emit_cache
    ncol = N_ZT + N_GT
    if weights_outer:
        def span(first, last):
            def rows_of(j, i):
                return jnp.where(j < first, 0, jnp.where(j <= last, i, nt - 1))
            return rows_of
        grid = (ncol, nt)
        x_map = lambda j, i: (jnp.where(j == 0, i, nt - 1), 0)
        z_map = lambda j, i: (span(0, N_ZT - 1)(j, i), jnp.minimum(j, N_ZT - 1))
        g_map = lambda j, i: (span(N_ZT, ncol - 1)(j, i), jnp.maximum(j - N_ZT, 0))
        cache_maps = [lambda j, i, rows_of=span(tile, tile): (rows_of(j, i), l, 0, 0) for tile in (1, 2, 4, 5)]
        w_map = lambda j, i: (l, 0, j)
        const = lambda j, i: (l, 0, 0)
        mod_specs = [pl.BlockSpec((1, 1, D), lambda j, i, which=which: (l * 16 + row_of_tile(i), 0, which))
                     for which in (0, 1)]
    else:
        grid = (nt, ncol)
        x_map = lambda i, j: (i, 0)
        z_map = lambda i, j: (i, jnp.minimum(j, N_ZT - 1))
        g_map = lambda i, j: (i, jnp.maximum(j - N_ZT, 0))
        cache_maps = [lambda i, j: (i, l, 0, 0)] * 4
        w_map = lambda i, j: (l, 0, j)
        const = lambda i, j: (l, 0, 0)
        mod_specs = [_mod_spec(l, 0, row_of_tile, 2), _mod_spec(l, 1, row_of_tile, 2)]
    in_specs = [pl.BlockSpec((tm, D), x_map), *mod_specs,
                pl.BlockSpec((1, 1, D), const),
                pl.BlockSpec((1, D, PROJ_TN), w_map)]
    args = [x, mods, mods, norm_g, w_in]
    if rope:
        in_specs += [pl.BlockSpec((tm, 128), lambda i, j: (0, 0))] * 2
        args += list(rope_tabs)
    out_specs = [pl.BlockSpec((tm, PROJ_TN), z_map), pl.BlockSpec((tm, PROJ_TN), g_map)]
    out_shape = [jax.ShapeDtypeStruct((t, Z_W), BF16), jax.ShapeDtypeStruct((t, G_W), BF16)]
    aliases = {}
    if emit_cache:
        nb = tm // 256
        out_specs += [pl.BlockSpec((nb, 1, 256, PROJ_TN), m) for m in cache_maps]
        out_shape += [jax.ShapeDtypeStruct((n_ctx, DEPTH, 256, PROJ_TN), F32)] * 4
        if caches is not None:
            base = len(args)
            in_specs += [pl.BlockSpec(memory_space=pl.ANY)] * 4
            args += list(caches)
            aliases = {base + a: 2 + a for a in range(4)}
    kw = dict(tm=tm, rope=rope, emit_cache=emit_cache, weights_outer=weights_outer)
    kern = functools.partial(_proj_kernel, **kw)
    if emit_cache and caches is not None:
        def kern(*refs):
            n_in = len(args)
            return _proj_kernel(*refs[:n_in - 4], *refs[n_in:], **kw)
    return pl.pallas_call(
        kern,
        grid=grid,
        in_specs=in_specs, out_specs=out_specs, out_shape=out_shape,
        scratch_shapes=[pltpu.VMEM((t if weights_outer else tm, D), BF16)],
        input_output_aliases=aliases,
        compiler_params=_params(("arbitrary", "arbitrary")),
        name="in_proj_ctx" if emit_cache else "in_proj_lat",
    )(*args)


ONES_ROWS = 16

def _diff_lambda(dl_ref, lam_init):
    dl = dl_ref[0]
    la = jnp.sum(dl[0:1] * dl[1:2], axis=-1, keepdims=True)
    lb = jnp.sum(dl[2:3] * dl[3:4], axis=-1, keepdims=True)
    return jnp.exp(la) - jnp.exp(lb) + lam_init


def _diff_lat_kernel(q_ref, k_ref, v_ref, ck_ref, cv_ref, dl_ref, gain_ref, o_ref, vt_s, *, n, lam_init):
    @pl.when(pl.program_id(2) == 0)
    def _():
        vt_s[:, :n] = v_ref[0].astype(F32).T.astype(BF16)
        vt_s[:, n:] = cv_ref[0, 0].T.astype(BF16)

    lo, hi = _half_masks(BF16, A_DH ** -0.5)
    kk = k_ref[0]
    ck = ck_ref[0, 0].astype(BF16)
    lam = _diff_lambda(dl_ref, lam_init)
    q = q_ref[0]
    es, dens = [], []
    for msk in (lo, hi):
        qh = q * msk
        s = _dot_nt(kk, qh)
        s2 = _dot_nt(ck, qh)
        m = jnp.maximum(jnp.max(s, axis=0, keepdims=True), jnp.max(s2, axis=0, keepdims=True))
        e = jnp.exp(s - m)
        e2 = jnp.exp(s2 - m)
        es.append((e, e2))
        dens.append(jnp.sum(e, axis=0, keepdims=True) + jnp.sum(e2, axis=0, keepdims=True))
    r = lam * dens[0] / dens[1]
    a = (es[0][0] - r * es[1][0]).astype(BF16)
    a2 = (es[0][1] - r * es[1][1]).astype(BF16)
    ot = (_dot(vt_s[:, :n], a) + _dot(vt_s[:, n:], a2)) / dens[0]
    on = ot * lax.rsqrt(jnp.mean(ot * ot, axis=0, keepdims=True) + EPS)
    o_ref[0] = ((on * gain_ref[0]) * (1.0 - lam_init)).T.astype(BF16)


def _diff_attention_lat(z3, cache_k, cache_v, diff_lambda, diff_norm_col, l, *, tq):
    b, n, _ = z3.shape
    past = cache_k.shape[2]
    lam_init = 0.8 - 0.6 * math.exp(-0.3 * l)
    return pl.pallas_call(
        functools.partial(_diff_lat_kernel, n=n, lam_init=lam_init),
        grid=(b, 4, n // tq),
        in_specs=[pl.BlockSpec((1, tq, 128), lambda bi, h, qi: (bi, qi, COL_AQ + h)),
                  pl.BlockSpec((1, n, 128), lambda bi, h, qi: (bi, 0, COL_AK + h)),
                  pl.BlockSpec((1, n, 128), lambda bi, h, qi: (bi, 0, COL_AV + h)),
                  pl.BlockSpec((1, 1, past, 128), lambda bi, h, qi: (bi, l, 0, h)),
                  pl.BlockSpec((1, 1, past, 128), lambda bi, h, qi: (bi, l, 0, h)),
                  pl.BlockSpec((1, 4, A_DH), lambda bi, h, qi: (l, 0, 0)),
                  pl.BlockSpec((1, 128, 1), lambda bi, h, qi: (l, h, 0))],
        out_specs=pl.BlockSpec((1, tq, 128), lambda bi, h, qi: (bi, qi, h)),
        out_shape=jax.ShapeDtypeStruct((b, n, 512), BF16),
        scratch_shapes=[pltpu.VMEM((128, n + past), BF16)],
        compiler_params=_params(("arbitrary",) * 3),
        name="diff_attn_lat",
    )(z3, z3, z3, cache_k, cache_v, diff_lambda, diff_norm_col)


def _ctx_attn_kernel(aq_ref, ak_ref, av_ref, bq_ref, bk_ref, bv_ref, dl_ref, gain_ref, ao_ref, bo_ref,
                     *, lam_init):
    lo, hi = _half_masks(BF16, A_DH ** -0.5)
    one_lo, one_hi = _half_masks(BF16)
    lam = _diff_lambda(dl_ref, lam_init)
    for h in range(4):
        hs = slice(h * 128, h * 128 + 128)
        q = aq_ref[0, :, hs]
        kk = ak_ref[0, :, hs]
        v = av_ref[0, :, hs]
        outs = []
        for msk in (lo, hi):
            s = _dot_nt(q * msk, kk)
            e = jnp.exp(s - jnp.max(s, axis=-1, keepdims=True))
            outs.append(_dot(e.astype(BF16), v) / jnp.sum(e, axis=-1, keepdims=True))
        o = outs[0] - lam * outs[1]
        on = o * lax.rsqrt(jnp.mean(o * o, axis=-1, keepdims=True) + EPS)
        ao_ref[0, :, hs] = ((on * gain_ref[0, :, hs]) * (1.0 - lam_init)).astype(BF16)

        q = bq_ref[0, :, hs]
        kk = bk_ref[0, :, hs]
        v = bv_ref[0, :, hs]
        acc = None
        for msk, vmsk in ((lo, one_lo), (hi, one_hi)):
            s = _dot_nt(q * msk, kk)
            e = jnp.exp(s - jnp.max(s, axis=-1, keepdims=True))
            o = _dot(e.astype(BF16), v * vmsk) / jnp.sum(e, axis=-1, keepdims=True)
            acc = o if acc is None else acc + o
        bo_ref[0, :, hs] = acc.astype(BF16)


def _ctx_attention(z3, diff_lambda, diff_norm, l):
    b, n, _ = z3.shape
    lam_init = 0.8 - 0.6 * math.exp(-0.3 * l)
    col = lambda c: pl.BlockSpec((1, n, 512), lambda bi: (bi, 0, c))
    return pl.pallas_call(
        functools.partial(_ctx_attn_kernel, lam_init=lam_init),
        grid=(b,),
        in_specs=[col(0), col(1), col(2), col(3), col(4), col(5),
                  pl.BlockSpec((1, 4, A_DH), lambda bi: (l, 0, 0)),
                  pl.BlockSpec((1, 1, 512), lambda bi: (l, 0, 0))],
        out_specs=[pl.BlockSpec((1, n, 512), lambda bi: (bi, 0, 0))] * 2,
        out_shape=[jax.ShapeDtypeStruct((b, n, 512), BF16)] * 2,
        compiler_params=_params(("arbitrary",)),
        name="attn_ctx",
    )(z3, z3, z3, z3, z3, z3, diff_lambda, diff_norm)


def _na_table_kernel(rpb_ref, o_ref):
    l = pl.program_id(0)
    h = pl.program_id(1)
    nrel_r = 2 * NA_ROWS - 1
    nrel_c = 2 * NA_COLS - 1
    base = (l * 8 + h) * (nrel_r * nrel_c)
    kc = lax.broadcasted_iota(I32, (GRID_W, GRID_W), 0)
    qc = lax.broadcasted_iota(I32, (GRID_W, GRID_W), 1)
    dcm = jnp.clip(kc - qc + (NA_COLS - 1), 0, nrel_c - 1)
    ws = jnp.clip(qc - NA_COLS // 2, 0, GRID_W - NA_COLS)
    col_ok = (kc >= ws) & (kc < ws + NA_COLS)
    neg = jnp.full((GRID_W, GRID_W), NEG_INF, F32)
    tabs = []
    for dr in range(nrel_r):
        acc = jnp.zeros((GRID_W, GRID_W), F32)
        for dc in range(nrel_c):
            acc = jnp.where(dcm == dc, rpb_ref[base + dr * nrel_c + dc], acc)
        tabs.append(jnp.where(col_ok, acc, neg))
    for typ in range(3):
        for i in range(NA_QR):
            for j in range(NA_KR):
                if typ == 0:
                    valid, dr = j < NA_ROWS, j - i + 7
                elif typ == 1:
                    valid, dr = i <= j < i + NA_ROWS, j - i + 3
                else:
                    valid, dr = j >= 4, j - i - 1
                o_ref[0, 0, typ, j * 64:(j + 1) * 64, i * 64:(i + 1) * 64] = tabs[dr] if valid else neg


def _na_tables(na_rpb):
    return pl.pallas_call(
        _na_table_kernel,
        grid=(DEPTH, 8),
        in_specs=[pl.BlockSpec(memory_space=pltpu.SMEM)],
        out_specs=pl.BlockSpec((1, 1, 3, NA_KR * 64, NA_QR * 64), lambda l, h: (l, h, 0, 0, 0)),
        out_shape=jax.ShapeDtypeStruct((DEPTH, 8, 3, NA_KR * 64, NA_QR * 64), F32),
        compiler_params=_params(("arbitrary",) * 2),
        name="na_bias_tables",
    )(na_rpb.reshape(-1))


def _na_kernel(q_ref, k_ref, v_ref, ck_ref, cv_ref, tab_ref, o_ref, vt_s, cvt_s, slot_a, slot_b, *, n, nrb):
    rb = pl.program_id(2)
    nwin = NA_KR * GRID_W

    @pl.when(rb == 0)
    def _():
        for pr in range(2):
            ps = slice(pr * 128, pr * 128 + 128)
            vt = v_ref[0, :, ps].astype(F32).T
            for c in range(n // 128):
                vt_s[pr, c] = vt[:, c * 128:(c + 1) * 128].astype(BF16)
            cvt_s[pr] = cv_ref[0, 0, :, ps].T.astype(BF16)

    def first_key_row(block):
        return jnp.clip(block * NA_QR - NA_HALF, 0, 32 - NA_KR)

    def scores(dst):
        typ = jnp.where(rb == 0, 0, jnp.where(rb == nrb - 1, 2, 1))
        off = pl.multiple_of(first_key_row(rb) * GRID_W, 256)
        lo, hi = _half_masks(BF16, 0.125)
        for pr in range(2):
            ps = slice(pr * 128, pr * 128 + 128)
            q = q_ref[0, :, ps]
            kw = k_ref[0, pl.ds(off, nwin), ps]
            ck = ck_ref[0, 0, :, ps].astype(BF16)
            for half, msk in enumerate((lo, hi)):
                qh = q * msk
                dst[pr * 2 + half, :nwin, :] = _dot_nt(kw, qh) + tab_ref[0, pr * 2 + half, typ]
                dst[pr * 2 + half, nwin:, :] = _dot_nt(ck, qh)

    def finish(src):
        c0 = first_key_row(rb - 1) // 2
        for pr in range(2):
            vtw = jnp.concatenate([vt_s[pr, c0 + c] for c in range(nwin // 128)] + [cvt_s[pr]], axis=1)
            outs = []
            for half in range(2):
                s = src[pr * 2 + half]
                e = jnp.exp(s - jnp.max(s, axis=0, keepdims=True)).astype(BF16)
                lhs = jnp.concatenate([vtw[half * 64:half * 64 + 64],
                                       jnp.ones((ONES_ROWS, vtw.shape[1]), BF16)], axis=0)
                acc = _dot(lhs, e)
                outs.append(acc[:64] / acc[64:65])
            o_ref[0, :, pr * 128:pr * 128 + 128] = jnp.concatenate(outs, axis=0).T.astype(BF16)

    even = rb % 2 == 0

    @pl.when(rb == 0)
    def _():
        scores(slot_a)

    @pl.when((rb > 0) & (rb < nrb) & even)
    def _():
        scores(slot_a)
        finish(slot_b)

    @pl.when((rb < nrb) & jnp.logical_not(even))
    def _():
        scores(slot_b)
        finish(slot_a)

    @pl.when(rb == nrb)
    def _():
        finish(slot_b if nrb % 2 == 0 else slot_a)


def _na_attention(z3, cache_k, cache_v, tables, l):
    b, n, _ = z3.shape
    past = cache_k.shape[2]
    tq = NA_QR * GRID_W
    nrb = n // tq
    slot = pltpu.VMEM((4, NA_KR * GRID_W + past, tq), F32)
    return pl.pallas_call(
        functools.partial(_na_kernel, n=n, nrb=nrb),
        grid=(2, b, nrb + 1),
        in_specs=[pl.BlockSpec((1, tq, 256), lambda pp, bi, rb: (bi, jnp.minimum(rb, nrb - 1), COL_BQ // 2 + pp)),
                  pl.BlockSpec((1, n, 256), lambda pp, bi, rb: (bi, 0, COL_BK // 2 + pp)),
                  pl.BlockSpec((1, n, 256), lambda pp, bi, rb: (bi, 0, COL_BV // 2 + pp)),
                  pl.BlockSpec((1, 1, past, 256), lambda pp, bi, rb: (bi, l, 0, pp)),
                  pl.BlockSpec((1, 1, past, 256), lambda pp, bi, rb: (bi, l, 0, pp)),
                  pl.BlockSpec((1, 4, 3, NA_KR * 64, tq), lambda pp, bi, rb: (l, pp, 0, 0, 0))],
        out_specs=pl.BlockSpec((1, tq, 256), lambda pp, bi, rb: (bi, jnp.maximum(rb - 1, 0), pp)),
        out_shape=jax.ShapeDtypeStruct((b, n, 512), BF16),
        scratch_shapes=[pltpu.VMEM((2, n // 128, 128, 128), BF16), pltpu.VMEM((2, 128, past), BF16),
                        slot, slot],
        compiler_params=_params(("arbitrary",) * 3),
        name="na_attn_lat",
    )(z3, z3, z3, cache_k, cache_v, tables)


RET_UNROLL = 16


def _ret_kernel(*refs, **kw):
    for h in range(4):
        _ret_head(*refs, h=h, **kw)


def _ret_head(*refs, h, n, l, has_state, emit_state):
    rd_ref, q_ref, k_ref, v_ref, g_ref, gain_ref = refs[:6]
    i = 6
    if has_state:
        s0f_ref, s0b_ref = refs[i:i + 2]
        i += 2
    o_ref = refs[i]
    i += 1
    if emit_state:
        sf_ref, sb_ref = refs[i:i + 2]
        i += 2
    uf, ub = refs[i:i + 2]
    t = RET_CHUNK
    nc = n // t
    unroll = min(nc, RET_UNROLL)
    half = h % 2
    pair = slice((h // 2) * 128, (h // 2) * 128 + 128)
    head = slice(h * 128, h * 128 + 128)

    def log_gamma(direction):
        x = jnp.full((1, 1), rd_ref[l * 8 + direction * 4 + h], F32)
        return jnp.minimum(x, 0.0) - jnp.log1p(jnp.exp(-jnp.abs(x)))

    lgf = log_gamma(0)
    lgb = log_gamma(1)
    lane = lax.broadcasted_iota(I32, (1, 128), 1)
    hmask = jnp.where((lane >= 64) if half else (lane < 64), 1.0, 0.0)
    ri = lax.broadcasted_iota(I32, (t, t), 0).astype(F32)
    ci = lax.broadcasted_iota(I32, (t, t), 1).astype(F32)
    diff = ri - ci
    dmat = jnp.where(diff > 0, jnp.exp(lgf * jnp.maximum(diff, 0.0)),
                     jnp.where(diff < 0, jnp.exp(lgb * jnp.maximum(-diff, 0.0)), 2.0))
    pos_r = lax.broadcasted_iota(I32, (1, t), 1).astype(F32)
    pos_c = lax.broadcasted_iota(I32, (t, 1), 0).astype(F32)
    upd_f = jnp.exp(lgf * (t - 1.0 - pos_r))
    upd_b = jnp.exp(lgb * pos_r)
    cross_f = jnp.exp(lgf * (pos_c + 1.0))
    cross_b = jnp.exp(lgb * (t - pos_c))
    gch_f = jnp.exp(lgf * t)
    gch_b = jnp.exp(lgb * t)
    kscale = hmask * (64 ** -0.5)

    def chunk(ref, c, lanes):
        return ref[0, pl.ds(pl.multiple_of(c * t, t), t), lanes]

    def summarise(c, carry):
        kt = (chunk(k_ref, c, pair).astype(F32) * kscale).T
        vc = chunk(v_ref, c, head)
        uf[c] = _dot((kt * upd_f).astype(BF16), vc)
        ub[c] = _dot((kt * upd_b).astype(BF16), vc)
        return carry
    lax.fori_loop(0, nc, summarise, 0, unroll=unroll)

    if has_state:
        s0f = jnp.concatenate([s0f_ref[0, 0, h], s0f_ref[0, 0, h]], axis=0)
        s0b = jnp.concatenate([s0b_ref[0, 0, h], s0b_ref[0, 0, h]], axis=0)
    else:
        s0f = jnp.zeros((128, 128), F32)
        s0b = s0f

    def scan_f(c, s):
        u = uf[c]
        uf[c] = s
        return gch_f * s + u
    s_fin_f = lax.fori_loop(0, nc, scan_f, s0f, unroll=True)

    def scan_b(cc, s):
        c = nc - 1 - cc
        u = ub[c]
        ub[c] = s
        return gch_b * s + u
    s_fin_b = lax.fori_loop(0, nc, scan_b, s0b, unroll=True)

    if emit_state:
        sf_ref[0, 0, h] = s_fin_f[half * 64:half * 64 + 64]
        sb_ref[0, 0, h] = s_fin_b[half * 64:half * 64 + 64]

    gain = gain_ref[0, :, head]

    def emit(c, carry):
        qc = (chunk(q_ref, c, pair).astype(F32) * hmask).astype(BF16)
        kc = (chunk(k_ref, c, pair).astype(F32) * kscale).astype(BF16)
        vc = chunk(v_ref, c, head)
        att = (_dot_nt(qc, kc) * dmat).astype(BF16)
        o = (_dot(att, vc)
             + _dot(qc, uf[c].astype(BF16)) * cross_f
             + _dot(qc, ub[c].astype(BF16)) * cross_b)
        mu = jnp.mean(o, axis=-1, keepdims=True)
        var = jnp.mean(jnp.square(o - mu), axis=-1, keepdims=True)
        on = (o - mu) * lax.rsqrt(var + EPS) * gain
        g = chunk(g_ref, c, head).astype(F32)
        o_ref[0, pl.ds(pl.multiple_of(c * t, t), t), head] = (g * jax.nn.sigmoid(g) * on).astype(BF16)
        return carry
    lax.fori_loop(0, nc, emit, 0, unroll=unroll)


def _retention(z3, ret_decay, ret_norm, l, *, s0f=None, s0b=None, states=None, emit_state=False):
    b, n, _ = z3.shape
    has_state = s0f is not None
    in_specs = [pl.BlockSpec(memory_space=pltpu.SMEM),
                pl.BlockSpec((1, n, 256), lambda bi: (bi, 0, COL_CQ // 2)),
                pl.BlockSpec((1, n, 256), lambda bi: (bi, 0, COL_CK // 2)),
                pl.BlockSpec((1, n, 512), lambda bi: (bi, 0, COL_CV // 4)),
                pl.BlockSpec((1, n, 512), lambda bi: (bi, 0, COL_CG // 4)),
                pl.BlockSpec((1, 1, 512), lambda bi: (l, 0, 0))]
    args = [ret_decay.reshape(-1), z3, z3, z3, z3, ret_norm]
    if has_state:
        in_specs += [pl.BlockSpec((1, 1, 4, 64, 128), lambda bi: (bi, l, 0, 0, 0))] * 2
        args += [s0f, s0b]
    out_specs = [pl.BlockSpec((1, n, 512), lambda bi: (bi, 0, 0))]
    out_shape = [jax.ShapeDtypeStruct((b, n, 512), BF16)]
    aliases = {}
    n_in = len(args)
    if emit_state:
        out_specs += [pl.BlockSpec((1, 1, 4, 64, 128), lambda bi: (bi, l, 0, 0, 0))] * 2
        out_shape += [jax.ShapeDtypeStruct((b, DEPTH, 4, 64, 128), F32)] * 2
        if states is not None:
            in_specs += [pl.BlockSpec(memory_space=pl.ANY)] * 2
            args += list(states)
            aliases = {n_in: 1, n_in + 1: 2}
    kw = dict(n=n, l=l, has_state=has_state, emit_state=emit_state)

    def kern(*refs):
        return _ret_kernel(*refs[:n_in], *refs[len(args):], **kw)
    return pl.pallas_call(
        kern,
        grid=(b,),
        in_specs=in_specs, out_specs=out_specs, out_shape=out_shape,
        scratch_shapes=[pltpu.VMEM((n // RET_CHUNK, 128, 128), F32)] * 2,
        input_output_aliases=aliases,
        compiler_params=_params(("arbitrary",)),
        name="retention_ctx" if emit_state else "retention_lat",
    )(*args)


def _merge_kernel(x_ref, a_ref, b_ref, c_ref, gt_ref, wa_ref, wb_ref, wc_ref, wo_ref,
                  g1_ref, sh_ref, sc_ref, ng_ref, wr_ref, xo_ref, h_ref, lg_ref,
                  wa_s, wb_s, wc_s, wo_s, wr_s):
    @pl.when(pl.program_id(0) == 0)
    def _():
        wa_s[...] = wa_ref[0].astype(BF16)
        wb_s[...] = wb_ref[0].astype(BF16)
        wc_s[...] = wc_ref[0].astype(BF16)
        wo_s[...] = wo_ref[0].astype(BF16)
        wr_s[...] = wr_ref[0].astype(BF16)

    merged = None
    for br, (o_ref, w_s) in enumerate(((a_ref, wa_s), (b_ref, wb_s), (c_ref, wc_s))):
        gate = jax.nn.sigmoid(gt_ref[:, br * D:(br + 1) * D].astype(F32))
        term = gate * _dot(o_ref[...], w_s[...])
        merged = term if merged is None else merged + term
    x = x_ref[...] + g1_ref[0] * _dot(merged.astype(BF16), wo_s[...])
    xo_ref[...] = x
    y = x * lax.rsqrt(jnp.mean(x * x, axis=-1, keepdims=True) + EPS) * ng_ref[0]
    h = (y * (1.0 + sc_ref[0]) + sh_ref[0]).astype(BF16)
    h_ref[...] = h
    lg_ref[...] = _dot_nt(wr_s[...], h)


def _merge(x, a_o, b_o, c_o, gates, mods, norm_ffn, w_pa, w_pb, w_pc, w_out, w_router_t, l, *, tm, row_of_tile):
    t = x.shape[0]
    tok = lambda w: pl.BlockSpec((tm, w), lambda i: (i, 0))
    wspec = lambda r, c: pl.BlockSpec((1, r, c), lambda i: (l, 0, 0))
    return pl.pallas_call(
        _merge_kernel,
        grid=(t // tm,),
        in_specs=[tok(D), tok(512), tok(512), tok(512), tok(G_W),
                  wspec(512, D), wspec(512, D), wspec(512, D), wspec(D, D),
                  _mod_spec(l, 2, row_of_tile, 1), _mod_spec(l, 3, row_of_tile, 1),
                  _mod_spec(l, 4, row_of_tile, 1),
                  pl.BlockSpec((1, 1, D), lambda i: (l, 0, 0)), wspec(N_EXPERTS, D)],
        out_specs=[tok(D), tok(D), pl.BlockSpec((N_EXPERTS, tm), lambda i: (0, i))],
        out_shape=[jax.ShapeDtypeStruct((t, D), F32), jax.ShapeDtypeStruct((t, D), BF16),
                   jax.ShapeDtypeStruct((N_EXPERTS, t), F32)],
        scratch_shapes=[pltpu.VMEM((512, D), BF16)] * 3 + [pltpu.VMEM((D, D), BF16),
                                                            pltpu.VMEM((N_EXPERTS, D), BF16)],
        compiler_params=_params(("arbitrary",)),
        name="merge_out_proj",
    )(x, a_o, b_o, c_o, gates, w_pa, w_pb, w_pc, w_out, mods, mods, mods, norm_ffn, w_router_t)


def _route_kernel(lg_ref, h_ref, xg_ref, gs_ref, pos_ref, aff_s, *, nseg):
    seg_len = TOK_TILE // nseg
    cap = seg_len // 8
    cps = seg_len // 256
    lg = lg_ref[...]
    e = jnp.exp(lg - jnp.max(lg, axis=0, keepdims=True))
    aff = e / jnp.sum(e, axis=0, keepdims=True)
    aff_s[...] = aff

    def seg_sums(x):
        return [jnp.sum(x[:, s * seg_len:(s + 1) * seg_len], axis=1, keepdims=True) for s in range(nseg)]

    def seg_spread(vals):
        return jnp.concatenate([jnp.broadcast_to(v, (N_EXPERTS, seg_len)) for v in vals], axis=1)

    def search(it, thr):
        cand = thr | jnp.left_shift(jnp.int32(1), 30 - it)
        ge = aff >= lax.bitcast_convert_type(cand, F32)
        cnt = seg_spread(seg_sums(jnp.where(ge, 1.0, 0.0)))
        return jnp.where(cnt >= cap, cand, thr)
    thr_bits = lax.fori_loop(0, 31, search, jnp.zeros((N_EXPERTS, TOK_TILE), I32))
    thr = lax.bitcast_convert_type(thr_bits, F32)
    thr_next = lax.bitcast_convert_type(thr_bits + 1, F32)

    r = lax.broadcasted_iota(I32, (256, 256), 0)
    c = lax.broadcasted_iota(I32, (256, 256), 1)
    tri = jnp.where(r < c, 1.0, 0.0).astype(BF16)

    def excl_cumsum(x):
        outs = []
        run = None
        for ch in range(TOK_TILE // 256):
            xc = x[:, ch * 256:(ch + 1) * 256]
            p = _dot(xc.astype(BF16), tri)
            if cps > 1:
                if ch % cps == 0:
                    run = jnp.zeros((N_EXPERTS, 1), F32)
                p = p + run
                run = run + jnp.sum(xc, axis=1, keepdims=True)
            outs.append(p)
        return jnp.concatenate(outs, axis=1)

    gt = jnp.where(aff >= thr_next, 1.0, 0.0)
    eq = jnp.where(aff >= thr, 1.0, 0.0) - gt
    need = cap - seg_spread(seg_sums(gt))
    sel = gt + eq * jnp.where(excl_cumsum(eq) < need, 1.0, 0.0)
    seg_base = seg_spread([jnp.full((N_EXPERTS, 1), float(s * cap), F32) for s in range(nseg)])
    slot = (excl_cumsum(sel) + seg_base).astype(I32)
    pos_ref[...] = jnp.where(sel > 0.5, slot, -1)

    slot_iota = lax.broadcasted_iota(I32, (SLOTS, TOK_TILE), 0)
    h = h_ref[...]
    for grp in range(N_EXPERTS // 4):
        onehots = []
        for ex in range(grp * 4, grp * 4 + 4):
            hit = slot_iota == pos_ref[ex:ex + 1, :]
            gs_ref[ex] = jnp.sum(jnp.where(hit, aff_s[ex:ex + 1, :], 0.0), axis=1, keepdims=True)
            onehots.append(jnp.where(hit, 1.0, 0.0).astype(BF16))
        xg = _dot(jnp.concatenate(onehots, axis=0), h)
        xg_ref[grp * 4:grp * 4 + 4] = xg.astype(BF16).reshape(4, SLOTS, D)


def _route(logits_t, h, *, nseg):
    t = h.shape[0]
    nt = t // TOK_TILE
    return pl.pallas_call(
        functools.partial(_route_kernel, nseg=nseg),
        grid=(nt,),
        in_specs=[pl.BlockSpec((N_EXPERTS, TOK_TILE), lambda i: (0, i)),
                  pl.BlockSpec((TOK_TILE, D), lambda i: (i, 0))],
        out_specs=[pl.BlockSpec((N_EXPERTS, SLOTS, D), lambda i: (0, i, 0)),
                   pl.BlockSpec((N_EXPERTS, SLOTS, 1), lambda i: (0, i, 0)),
                   pl.BlockSpec((N_EXPERTS, TOK_TILE), lambda i: (0, i))],
        out_shape=[jax.ShapeDtypeStruct((N_EXPERTS, nt * SLOTS, D), BF16),
                   jax.ShapeDtypeStruct((N_EXPERTS, nt * SLOTS, 1), F32),
                   jax.ShapeDtypeStruct((N_EXPERTS, t), I32)],
        scratch_shapes=[pltpu.VMEM((N_EXPERTS, TOK_TILE), F32)],
        compiler_params=_params(("arbitrary",)),
        name="route_gather",
    )(logits_t, h)


FFN_SPLIT = 2
FFN_CHUNK = 512


def _ffn_kernel(xc_ref, xl_ref, gc_ref, gl_ref, wg_ref, wu_ref, wd_ref, oc_ref, ol_ref, wg_s, wu_s, wd_s):
    @pl.when(pl.program_id(1) == 0)
    def _():
        wg_s[...] = wg_ref[0, 0].astype(BF16)
        wu_s[...] = wu_ref[0, 0].astype(BF16)
        wd_s[...] = wd_ref[0, 0].astype(BF16)

    for x_ref, g_ref, o_ref in ((xc_ref, gc_ref, oc_ref), (xl_ref, gl_ref, ol_ref)):
        rows = x_ref.shape[1]
        step = min(rows, FFN_CHUNK)
        for r0 in range(0, rows, step):
            rs = slice(r0, r0 + step)
            x = x_ref[0, rs, :]
            a = _dot(x, wg_s[...])
            hid = (a * jax.nn.sigmoid(a) * _dot(x, wu_s[...])).astype(BF16)
            o_ref[0, rs, :] = (_dot(hid, wd_s[...]) * g_ref[0, rs, :]).astype(BF16)


def _expert_ffn(xg_c, xg_l, gs_c, gs_l, w_gate, w_up, w_down, l):
    rc = xg_c.shape[1] // FFN_SPLIT
    rl = xg_l.shape[1] // FFN_SPLIT
    assert rc % 16 == 0 and rl % min(rl, FFN_CHUNK) == 0
    row_map = lambda e, m: (e, m, 0)
    wspec = lambda r, c: pl.BlockSpec((1, 1, r, c), lambda e, m: (l, e, 0, 0))
    return pl.pallas_call(
        _ffn_kernel,
        grid=(N_EXPERTS, FFN_SPLIT),
        in_specs=[pl.BlockSpec((1, rc, D), row_map), pl.BlockSpec((1, rl, D), row_map),
                  pl.BlockSpec((1, rc, 1), row_map), pl.BlockSpec((1, rl, 1), row_map),
                  wspec(D, EXPERT_FF), wspec(D, EXPERT_FF), wspec(EXPERT_FF, D)],
        out_specs=[pl.BlockSpec((1, rc, D), row_map), pl.BlockSpec((1, rl, D), row_map)],
        out_shape=[jax.ShapeDtypeStruct(xg_c.shape, BF16), jax.ShapeDtypeStruct(xg_l.shape, BF16)],
        scratch_shapes=[pltpu.VMEM((D, EXPERT_FF), BF16), pltpu.VMEM((D, EXPERT_FF), BF16),
                        pltpu.VMEM((EXPERT_FF, D), BF16)],
        compiler_params=_params(("arbitrary",) * 2),
        name="expert_ffn",
    )(xg_c, xg_l, gs_c, gs_l, w_gate, w_up, w_down)


SC_TM = 1024


def _scatter_kernel(x_ref, og_ref, pt_ref, g2_ref, ng_ref, o_ref, *, final):
    lane = lax.broadcasted_iota(I32, (SC_TM, SLOTS), 1)
    pt = pt_ref[...]
    onehot = jnp.concatenate(
        [jnp.where(lane == pt[:, ex:ex + 1], 1.0, 0.0).astype(BF16) for ex in range(N_EXPERTS)], axis=1)
    upd = _dot(onehot, og_ref[...].reshape(N_EXPERTS * SLOTS, D))
    x = x_ref[...] + g2_ref[0] * upd
    if final:
        x = x * lax.rsqrt(jnp.mean(x * x, axis=-1, keepdims=True) + EPS) * ng_ref[...]
    o_ref[...] = x


def _scatter(x, outg, pos_t, mods, norm_final, l, *, row_of_tile, final):
    t = x.shape[0]
    per = TOK_TILE // SC_TM
    return pl.pallas_call(
        functools.partial(_scatter_kernel, final=final),
        grid=(t // SC_TM,),
        in_specs=[pl.BlockSpec((SC_TM, D), lambda i: (i, 0)),
                  pl.BlockSpec((N_EXPERTS, SLOTS, D), lambda i: (0, i // per, 0)),
                  pl.BlockSpec((SC_TM, N_EXPERTS), lambda i: (i, 0)),
                  _mod_spec(l, 5, row_of_tile, 1),
                  pl.BlockSpec((1, D), lambda i: (0, 0))],
        out_specs=pl.BlockSpec((SC_TM, D), lambda i: (i, 0)),
        out_shape=jax.ShapeDtypeStruct((t, D), F32),
        compiler_params=_params(("arbitrary",)),
        name="scatter_residual",
    )(x, outg, pos_t, mods, norm_final)


def kernel(x_prompt, x_sample, cache_a_k, cache_a_v, cache_b_k, cache_b_v, state_c_fwd, state_c_bwd, c, c_ctx, w_mod, b_mod, norm_mix, norm_ffn, w_in, diff_lambda, diff_norm, na_rpb, ret_decay, ret_norm, w_proj_a, w_proj_b, w_proj_c, w_out, w_router, w_exp_gate, w_exp_up, w_exp_down, norm_final):
    nb_c, n_c, _ = x_prompt.shape
    nb_l, n_l, _ = x_sample.shape
    past = cache_a_k.shape[2]
    assert (n_c, n_l, past) == (256, TOK_TILE, 256) and nb_c % 8 == 0 and nb_l + 1 <= 16

    cvec = jnp.zeros((16, D), F32).at[0].set(c_ctx).at[1:1 + nb_l].set(c)
    mods = _modulation(cvec, w_mod, b_mod).reshape(DEPTH * 16, 1, 6 * D)
    tables = _na_tables(na_rpb)
    rope_tabs = _rope_tables(n_l)

    norm_mix3 = norm_mix.reshape(DEPTH, 1, D)
    norm_ffn3 = norm_ffn.reshape(DEPTH, 1, D)
    diff_norm3 = diff_norm.reshape(DEPTH, 1, 512)
    diff_norm_col = diff_norm.reshape(DEPTH, 512, 1)
    ret_norm3 = ret_norm.reshape(DEPTH, 1, 512)
    w_router_t = jnp.swapaxes(w_router, 1, 2)
    norm_final2 = norm_final.reshape(1, D)
    ca_k = cache_a_k.reshape(nb_l, DEPTH, past, 512)
    ca_v = cache_a_v.reshape(nb_l, DEPTH, past, 512)
    cb_k = cache_b_k.reshape(nb_l, DEPTH, past, 512)
    cb_v = cache_b_v.reshape(nb_l, DEPTH, past, 512)

    ctx_row = lambda i: 0
    lat_rows = {tm: (lambda i, per=n_l // tm: 1 + i // per) for tm in (512, SC_TM, TOK_TILE)}

    xc = x_prompt.reshape(nb_c * n_c, D)
    xl = x_sample.reshape(nb_l * n_l, D)
    caches = None
    states = None
    for l in range(DEPTH):
        last = l == DEPTH - 1
        outs = _project(xc, mods, norm_mix3, w_in, l, tm=1024, row_of_tile=ctx_row, caches=caches, n_ctx=nb_c)
        zc, gc, caches = outs[0], outs[1], outs[2:]
        zl, gl = _project(xl, mods, norm_mix3, w_in, l, tm=TOK_TILE, row_of_tile=lat_rows[TOK_TILE],
                          rope_tabs=rope_tabs)
        zc3 = zc.reshape(nb_c, n_c, Z_W)
        zl3 = zl.reshape(nb_l, n_l, Z_W)
        a_c, b_c = _ctx_attention(zc3, diff_lambda, diff_norm3, l)
        a_l = _diff_attention_lat(zl3, ca_k, ca_v, diff_lambda, diff_norm_col, l, tq=1024)
        b_l = _na_attention(zl3, cb_k, cb_v, tables, l)
        rc = _retention(zc3, ret_decay, ret_norm3, l, states=states, emit_state=True)
        c_c, states = rc[0], rc[1:]
        c_l = _retention(zl3, ret_decay, ret_norm3, l, s0f=state_c_fwd, s0b=state_c_bwd)[0]
        xc, hc, lgc = _merge(xc, a_c.reshape(-1, 512), b_c.reshape(-1, 512), c_c.reshape(-1, 512), gc, mods,
                             norm_ffn3, w_proj_a, w_proj_b, w_proj_c, w_out, w_router_t, l,
                             tm=512, row_of_tile=ctx_row)
        xl, hl, lgl = _merge(xl, a_l.reshape(-1, 512), b_l.reshape(-1, 512), c_l.reshape(-1, 512), gl, mods,
                             norm_ffn3, w_proj_a, w_proj_b, w_proj_c, w_out, w_router_t, l,
                             tm=512, row_of_tile=lat_rows[512])
        xg_c, gs_c, pos_c = _route(lgc, hc, nseg=TOK_TILE // n_c)
        xg_l, gs_l, pos_l = _route(lgl, hl, nseg=1)
        og_c, og_l = _expert_ffn(xg_c, xg_l, gs_c, gs_l, w_exp_gate, w_exp_up, w_exp_down, l)
        xc = _scatter(xc, og_c, pos_c.T, mods, norm_final2, l, row_of_tile=ctx_row, final=last)
        xl = _scatter(xl, og_l, pos_l.T, mods, norm_final2, l, row_of_tile=lat_rows[SC_TM], final=last)

    new_ak, new_av, new_bk, new_bv = caches
    return (xc.reshape(nb_c, n_c, D), xl.reshape(nb_l, n_l, D),
            new_ak.reshape(nb_c, DEPTH, n_c, 4, 2, A_DH), new_av.reshape(nb_c, DEPTH, n_c, 4, 128),
            new_bk.reshape(nb_c, DEPTH, n_c, 8, 64), new_bv.reshape(nb_c, DEPTH, n_c, 8, 64),
            states[0], states[1])
```
